```python
import math
import jax
import jax.numpy as jnp
from jax import lax
import numpy as np

D_MODEL = 2048
BATCH = 8
SEQ = 2048
DEPTH = 4

N_MIXERS = 3
N_LAYERS_GDN = (DEPTH + 2) // 3
N_LAYERS_SSM = (DEPTH + 1) // 3
N_LAYERS_DSA = DEPTH // 3

DN_ALPHA = (2.0 * DEPTH) ** 0.25
DN_BETA = (8.0 * DEPTH) ** -0.25
LN_EPS = 1e-5
RMS_EPS = 1e-6
NEG_INF = -1e30

GDN_HEAD_DIM = 128
GDN_HEADS = D_MODEL // GDN_HEAD_DIM
GDN_QK = GDN_HEADS * GDN_HEAD_DIM
GDN_V = GDN_HEADS * GDN_HEAD_DIM
GDN_CONV = 4
GDN_CONV_CH = 2 * GDN_QK + GDN_V
GDN_IN = 2 * GDN_QK + 2 * GDN_V + 2 * GDN_HEADS
GDN_CHUNK = 64

SSM_WIDTH = D_MODEL
SSM_GROUP = 16
SSM_GROUPS = SSM_WIDTH // SSM_GROUP
SSM_STATE = 64

ATT_HEAD_DIM = 128
ATT_HEADS = D_MODEL // ATT_HEAD_DIM
IDX_HEADS = 16
IDX_DIM = 128
TOPK_MAX = 256
Q_BLOCK = 128
DSA_IN = ATT_HEADS * ATT_HEAD_DIM + 2 * ATT_HEAD_DIM + IDX_HEADS * IDX_DIM + IDX_DIM + IDX_HEADS

MOE_GROUPS = 4
MOE_EXPERTS_PER_GROUP = 8
MOE_EXPERTS = MOE_GROUPS * MOE_EXPERTS_PER_GROUP
MOE_TOPK = 2
MOE_FF = D_MODEL // 4
MOE_BLOCK = 256

kernel_name = 'hybrid_gdn_s5_dsa_hmoe_deepnorm'


def layer_norm(x, g, b):
    xf = x.astype(jnp.float32)
    mu = jnp.mean(xf, axis=-1, keepdims=True)
    var = jnp.mean(jnp.square(xf - mu), axis=-1, keepdims=True)
    return ((xf - mu) * lax.rsqrt(var + LN_EPS) * g.astype(jnp.float32) + b.astype(jnp.float32)).astype(x.dtype)


def l2_normalize(x):
    return x * lax.rsqrt(jnp.sum(jnp.square(x), axis=-1, keepdims=True) + RMS_EPS)


def causal_depthwise_conv(x, w):
    width, ch = w.shape
    return lax.conv_general_dilated(x, w[:, None, :].astype(x.dtype), window_strides=(1,),
                                    padding=[(width - 1, 0)],
                                    dimension_numbers=('NWC', 'WIO', 'NWC'),
                                    feature_group_count=ch)


def gated_delta_rule_chunked(q, k, v, g, beta):
    bsz, seq, heads, dk = q.shape
    dv = v.shape[-1]
    c = GDN_CHUNK
    n = seq // c

    def to_chunks(a):
        a = a.reshape((bsz, n, c, heads) + a.shape[3:])
        return jnp.moveaxis(a, 3, 1)

    q = to_chunks(q * dk ** -0.5)
    k = to_chunks(k)
    v = to_chunks(v)
    g = to_chunks(g)
    beta = to_chunks(beta)
    g_cum = jnp.cumsum(g, axis=-1)
    lower = jnp.tril(jnp.ones((c, c), dtype=bool))
    strict = jnp.tril(jnp.ones((c, c), dtype=bool), -1)
    diff = g_cum[..., :, None] - g_cum[..., None, :]
    decay = jnp.where(lower, jnp.exp(jnp.where(lower, diff, 0.0)), 0.0)
    k_beta = k * beta[..., None]
    kk = jnp.einsum('bhnid,bhnjd->bhnij', k_beta, k)
    a_mat = jnp.where(strict, kk * decay, 0.0) + jnp.eye(c, dtype=jnp.float32)
    rhs = jnp.concatenate([v * beta[..., None], k_beta * jnp.exp(g_cum)[..., None]], axis=-1)
    sol = lax.linalg.triangular_solve(a_mat, rhs, left_side=True, lower=True)
    u, w = sol[..., :dv], sol[..., dv:]
    attn = jnp.where(lower, jnp.einsum('bhnid,bhnjd->bhnij', q, k) * decay, 0.0)
    q_dec = q * jnp.exp(g_cum)[..., None]
    k_dec = k * jnp.exp(g_cum[..., -1:] - g_cum)[..., None]
    chunk_dec = jnp.exp(g_cum[..., -1])
    xs = tuple(jnp.moveaxis(t, 2, 0) for t in (u, w, attn, q_dec, k_dec, chunk_dec))

    def step(state, inp):
        u_c, w_c, a_c, qd_c, kd_c, cd_c = inp
        v_new = u_c - jnp.einsum('bhcd,bhde->bhce', w_c, state)
        o_c = jnp.einsum('bhcd,bhde->bhce', qd_c, state) + jnp.einsum('bhcj,bhje->bhce', a_c, v_new)
        state = state * cd_c[..., None, None] + jnp.einsum('bhcd,bhce->bhde', kd_c, v_new)
        return state, o_c

    state0 = jnp.zeros((bsz, heads, dk, dv), jnp.float32)
    _, o = lax.scan(step, state0, xs)
    return jnp.transpose(o, (1, 0, 3, 2, 4)).reshape(bsz, seq, heads, dv)


def gated_deltanet_mixer(x, w_in, conv_w, a_log, dt_bias, norm_g, w_out):
    bsz, seq, _ = x.shape
    proj = x @ w_in
    qkv = proj[..., :GDN_CONV_CH]
    z = proj[..., GDN_CONV_CH:GDN_CONV_CH + GDN_V]
    a = proj[..., GDN_CONV_CH + GDN_V:GDN_CONV_CH + GDN_V + GDN_HEADS]
    b = proj[..., GDN_CONV_CH + GDN_V + GDN_HEADS:]
    qkv = jax.nn.silu(causal_depthwise_conv(qkv, conv_w)).astype(jnp.float32)
    q = l2_normalize(qkv[..., :GDN_QK].reshape(bsz, seq, GDN_HEADS, GDN_HEAD_DIM))
    k = l2_normalize(qkv[..., GDN_QK:2 * GDN_QK].reshape(bsz, seq, GDN_HEADS, GDN_HEAD_DIM))
    v = qkv[..., 2 * GDN_QK:].reshape(bsz, seq, GDN_HEADS, GDN_HEAD_DIM)
    g = -jnp.exp(a_log.astype(jnp.float32)) * jax.nn.softplus(a.astype(jnp.float32) + dt_bias.astype(jnp.float32))
    beta = jax.nn.sigmoid(b.astype(jnp.float32))
    o = gated_delta_rule_chunked(q, k, v, g, beta)
    o = o * lax.rsqrt(jnp.mean(jnp.square(o), axis=-1, keepdims=True) + RMS_EPS) * norm_g.astype(jnp.float32)
    o = o * jax.nn.silu(z.astype(jnp.float32).reshape(bsz, seq, GDN_HEADS, GDN_HEAD_DIM))
    return o.reshape(bsz, seq, GDN_V).astype(x.dtype) @ w_out


def s5_mixer(x, w_in, b_re, b_im, c_re, c_im, a_re, a_im, log_dt, d_skip, w_glu, b_glu, w_out):
    bsz, seq, _ = x.shape
    u = (x @ w_in).astype(jnp.float32)
    ug = u.reshape(bsz, seq, SSM_GROUPS, SSM_GROUP)
    f32 = jnp.float32
    a_re = a_re.astype(f32)
    a_im = a_im.astype(f32)
    b_re = b_re.astype(f32)
    b_im = b_im.astype(f32)
    c_re = c_re.astype(f32)
    c_im = c_im.astype(f32)
    dt = jnp.exp(log_dt.astype(f32))[:, None]
    mag = jnp.exp(a_re * dt)
    ang = a_im * dt
    lb_re = mag * jnp.cos(ang)
    lb_im = mag * jnp.sin(ang)
    den = jnp.square(a_re) + jnp.square(a_im)
    f_re = ((lb_re - 1.0) * a_re + lb_im * a_im) / den
    f_im = (lb_im * a_re - (lb_re - 1.0) * a_im) / den
    bb_re = f_re[..., None] * b_re - f_im[..., None] * b_im
    bb_im = f_re[..., None] * b_im + f_im[..., None] * b_re

    def combine(e1, e2):
        a1r, a1i, s1r, s1i = e1
        a2r, a2i, s2r, s2i = e2
        return (a2r * a1r - a2i * a1i, a2r * a1i + a2i * a1r,
                a2r * s1r - a2i * s1i + s2r, a2r * s1i + a2i * s1r + s2i)

    def per_sequence(u_seq):
        bu_re = jnp.einsum('tgc,gpc->tgp', u_seq, bb_re)
        bu_im = jnp.einsum('tgc,gpc->tgp', u_seq, bb_im)
        ar = jnp.broadcast_to(lb_re, bu_re.shape)
        ai = jnp.broadcast_to(lb_im, bu_re.shape)
        _, _, s_re, s_im = lax.associative_scan(combine, (ar, ai, bu_re, bu_im), axis=0)
        return jnp.einsum('tgp,gcp->tgc', s_re, c_re) - jnp.einsum('tgp,gcp->tgc', s_im, c_im)

    y = lax.map(per_sequence, ug).reshape(bsz, seq, SSM_WIDTH)
    y = jax.nn.gelu(y + d_skip.astype(f32) * u).astype(x.dtype)
    y = y * jax.nn.sigmoid(y @ w_glu + b_glu)
    return y @ w_out


def dsa_mixer(x, w_in, w_out):
    bsz, seq, _ = x.shape
    proj = x @ w_in
    o0 = ATT_HEADS * ATT_HEAD_DIM
    o1 = o0 + ATT_HEAD_DIM
    o2 = o1 + ATT_HEAD_DIM
    o3 = o2 + IDX_HEADS * IDX_DIM
    o4 = o3 + IDX_DIM
    q = proj[..., :o0].reshape(bsz, seq, ATT_HEADS, ATT_HEAD_DIM)
    k = proj[..., o0:o1]
    v = proj[..., o1:o2]
    qi = proj[..., o2:o3].reshape(bsz, seq, IDX_HEADS, IDX_DIM)
    ki = proj[..., o3:o4]
    wi = proj[..., o4:]
    n_sel = min(TOPK_MAX, seq // 4)
    nb = seq // Q_BLOCK
    slopes = 2.0 ** (-8.0 * jnp.arange(1, ATT_HEADS + 1, dtype=jnp.float32) / ATT_HEADS)
    key_pos = jnp.arange(seq, dtype=jnp.int32)

    def blocks(a):
        return jnp.swapaxes(a.reshape((bsz, nb, Q_BLOCK) + a.shape[2:]), 0, 1)

    starts = jnp.arange(nb, dtype=jnp.int32) * Q_BLOCK

    def attend_block(inp):
        q_b, qi_b, wi_b, start = inp
        t_pos = start + jnp.arange(Q_BLOCK, dtype=jnp.int32)
        dots = jnp.einsum('bqhd,bsd->bqhs', qi_b, ki).astype(jnp.float32) * IDX_DIM ** -0.5
        score = jnp.einsum('bqhs,bqh->bqs', jax.nn.relu(dots), wi_b.astype(jnp.float32) * IDX_HEADS ** -0.5)
        causal = key_pos[None, :] <= t_pos[:, None]
        score = jnp.where(causal[None], score, NEG_INF)
        _, idx = lax.top_k(score, n_sel)
        k_sel = jax.vmap(lambda kk, ii: kk[ii])(k, idx)
        v_sel = jax.vmap(lambda vv, ii: vv[ii])(v, idx)
        logits = jnp.einsum('bqhd,bqkd->bhqk', q_b, k_sel).astype(jnp.float32) * ATT_HEAD_DIM ** -0.5
        dist = (t_pos[None, :, None] - idx).astype(jnp.float32)
        logits = logits - slopes[None, :, None, None] * dist[:, None]
        valid = (idx <= t_pos[None, :, None])[:, None]
        p = jax.nn.softmax(jnp.where(valid, logits, NEG_INF), axis=-1)
        return jnp.einsum('bhqk,bqkd->bqhd', p.astype(v_sel.dtype), v_sel)

    o = lax.map(attend_block, (blocks(q), blocks(qi), blocks(wi), starts))
    o = jnp.swapaxes(o, 0, 1).reshape(bsz, seq, ATT_HEADS * ATT_HEAD_DIM)
    return o @ w_out


def routed_experts(xt, expert_id, gate, w_gate, w_up, w_down):
    n_tok, d = xt.shape
    n_exp = w_gate.shape[0]
    top = expert_id.shape[1]
    n_asg = n_tok * top
    flat_e = expert_id.reshape(-1).astype(jnp.int32)
    flat_tok = jnp.repeat(jnp.arange(n_tok, dtype=jnp.int32), top)
    flat_gate = gate.reshape(-1)
    order = jnp.argsort(flat_e)
    e_s = flat_e[order]
    tok_s = flat_tok[order]
    gate_s = flat_gate[order]
    counts = jax.ops.segment_sum(jnp.ones_like(flat_e), flat_e, num_segments=n_exp)
    padded = (counts + MOE_BLOCK - 1) // MOE_BLOCK * MOE_BLOCK
    pad_end = jnp.cumsum(padded)
    pad_start = pad_end - padded
    cnt_start = jnp.cumsum(counts) - counts
    dest = pad_start[e_s] + (jnp.arange(n_asg, dtype=jnp.int32) - cnt_start[e_s])
    n_blocks = (n_asg + n_exp * (MOE_BLOCK - 1) + MOE_BLOCK - 1) // MOE_BLOCK
    cap = n_blocks * MOE_BLOCK
    slot_tok = jnp.full((cap,), n_tok, jnp.int32).at[dest].set(tok_s)
    slot_gate = jnp.zeros((cap,), jnp.float32).at[dest].set(gate_s)
    block_start = jnp.arange(n_blocks, dtype=jnp.int32) * MOE_BLOCK
    block_expert = jnp.minimum(jnp.searchsorted(pad_end, block_start, side='right'), n_exp - 1)
    x_pad = jnp.concatenate([xt, jnp.zeros((1, d), xt.dtype)], axis=0)

    def expert_block(inp):
        tok_b, e = inp
        xb = x_pad[tok_b]
        h = jax.nn.silu(xb @ w_gate[e]) * (xb @ w_up[e])
        return h @ w_down[e]

    y_slots = lax.map(expert_block, (slot_tok.reshape(n_blocks, MOE_BLOCK), block_expert))
    y_slots = y_slots.reshape(cap, d) * slot_gate[:, None].astype(xt.dtype)
    return jnp.zeros((n_tok + 1, d), xt.dtype).at[slot_tok].add(y_slots)[:n_tok]


def hierarchical_moe(x, rg_w, rg_b, re_w, re_b, w_gate, w_up, w_down):
    bsz, seq, d = x.shape
    xt = x.reshape(-1, d)
    n_tok = xt.shape[0]
    p_group = jax.nn.softmax((xt @ rg_w).astype(jnp.float32) + rg_b.astype(jnp.float32), axis=-1)
    p_g, g_idx = lax.top_k(p_group, 1)
    e_logits = ((xt @ re_w).astype(jnp.float32) + re_b.astype(jnp.float32)).reshape(n_tok, MOE_GROUPS, MOE_EXPERTS_PER_GROUP)
    gi = jnp.broadcast_to(g_idx[:, :, None], (n_tok, 1, MOE_EXPERTS_PER_GROUP))
    e_sel = jnp.take_along_axis(e_logits, gi, axis=1)[:, 0]
    p_e = jax.nn.softmax(e_sel, axis=-1)
    top_p, top_i = lax.top_k(p_e, MOE_TOPK)
    gate = p_g * top_p / jnp.sum(top_p, axis=-1, keepdims=True)
    expert_id = g_idx * MOE_EXPERTS_PER_GROUP + top_i
    y = routed_experts(xt, expert_id, gate, w_gate, w_up, w_down)
    return y.reshape(bsz, seq, d)


def setup_inputs(seed: int = 0) -> dict:
    key = jax.random.key(seed)
    ks = iter(jax.random.split(key, 40))
    f32 = jnp.float32

    def nrm(shape, scale):
        return jax.random.normal(next(ks), shape, f32) * scale

    def unif(shape, lo, hi):
        return jax.random.uniform(next(ks), shape, f32, minval=lo, maxval=hi)

    d = D_MODEL
    na, nb, nc = N_LAYERS_GDN, N_LAYERS_SSM, N_LAYERS_DSA
    gdn_dt = jnp.exp(unif((na, GDN_HEADS), math.log(1e-3), math.log(1e-1)))
    return {
        'x': nrm((BATCH, SEQ, d), 1.0),
        'ln_g': 1.0 + nrm((DEPTH, 2, d), 0.02),
        'ln_b': nrm((DEPTH, 2, d), 0.02),
        'moe_rg_w': nrm((DEPTH, d, MOE_GROUPS), d ** -0.5),
        'moe_rg_b': nrm((DEPTH, MOE_GROUPS), 0.01),
        'moe_re_w': nrm((DEPTH, d, MOE_EXPERTS), d ** -0.5),
        'moe_re_b': nrm((DEPTH, MOE_EXPERTS), 0.01),
        'moe_w_gate': nrm((DEPTH, MOE_EXPERTS, d, MOE_FF), d ** -0.5),
        'moe_w_up': nrm((DEPTH, MOE_EXPERTS, d, MOE_FF), d ** -0.5),
        'moe_w_down': nrm((DEPTH, MOE_EXPERTS, MOE_FF, d), DN_BETA * MOE_FF ** -0.5),
        'gdn_w_in': nrm((na, d, GDN_IN), d ** -0.5),
        'gdn_conv_w': nrm((na, GDN_CONV, GDN_CONV_CH), GDN_CONV ** -0.5),
        'gdn_a_log': jnp.log(unif((na, GDN_HEADS), 1.0, 16.0)),
        'gdn_dt_bias': gdn_dt + jnp.log(-jnp.expm1(-gdn_dt)),
        'gdn_norm_g': 1.0 + nrm((na, GDN_HEAD_DIM), 0.02),
        'gdn_w_out': nrm((na, GDN_V, d), DN_BETA * GDN_V ** -0.5),
        'ssm_w_in': nrm((nb, d, SSM_WIDTH), d ** -0.5),
        'ssm_b_re': nrm((nb, SSM_GROUPS, SSM_STATE, SSM_GROUP), (2.0 * SSM_GROUP) ** -0.5),
        'ssm_b_im': nrm((nb, SSM_GROUPS, SSM_STATE, SSM_GROUP), (2.0 * SSM_GROUP) ** -0.5),
        'ssm_c_re': nrm((nb, SSM_GROUPS, SSM_GROUP, SSM_STATE), (2.0 * SSM_STATE) ** -0.5),
        'ssm_c_im': nrm((nb, SSM_GROUPS, SSM_GROUP, SSM_STATE), (2.0 * SSM_STATE) ** -0.5),
        'ssm_a_re': -0.5 + nrm((nb, SSM_GROUPS, SSM_STATE), 0.01),
        'ssm_a_im': math.pi * jnp.arange(SSM_STATE, dtype=f32) + nrm((nb, SSM_GROUPS, SSM_STATE), 0.01),
        'ssm_log_dt': unif((nb, SSM_GROUPS), math.log(1e-3), math.log(1e-1)),
        'ssm_d': nrm((nb, SSM_WIDTH), 1.0),
        'ssm_w_glu': nrm((nb, SSM_WIDTH, SSM_WIDTH), SSM_WIDTH ** -0.5),
        'ssm_b_glu': nrm((nb, SSM_WIDTH), 0.01),
        'ssm_w_out': nrm((nb, SSM_WIDTH, d), DN_BETA * SSM_WIDTH ** -0.5),
        'dsa_w_in': nrm((nc, d, DSA_IN), d ** -0.5),
        'dsa_w_out': nrm((nc, ATT_HEADS * ATT_HEAD_DIM, d), DN_BETA * (ATT_HEADS * ATT_HEAD_DIM) ** -0.5),
    }


def reference(x, ln_g, ln_b, moe_rg_w, moe_rg_b, moe_re_w, moe_re_b, moe_w_gate, moe_w_up, moe_w_down,
              gdn_w_in, gdn_conv_w, gdn_a_log, gdn_dt_bias, gdn_norm_g, gdn_w_out,
              ssm_w_in, ssm_b_re, ssm_b_im, ssm_c_re, ssm_c_im, ssm_a_re, ssm_a_im, ssm_log_dt,
              ssm_d, ssm_w_glu, ssm_b_glu, ssm_w_out, dsa_w_in, dsa_w_out):
    i_gdn = 0
    i_ssm = 0
    i_dsa = 0
    for layer in range(DEPTH):
        kind = layer % N_MIXERS
        if kind == 0:
            h = gated_deltanet_mixer(x, gdn_w_in[i_gdn], gdn_conv_w[i_gdn], gdn_a_log[i_gdn],
                                     gdn_dt_bias[i_gdn], gdn_norm_g[i_gdn], gdn_w_out[i_gdn])
            i_gdn += 1
        elif kind == 1:
            h = s5_mixer(x, ssm_w_in[i_ssm], ssm_b_re[i_ssm], ssm_b_im[i_ssm], ssm_c_re[i_ssm],
                         ssm_c_im[i_ssm], ssm_a_re[i_ssm], ssm_a_im[i_ssm], ssm_log_dt[i_ssm],
                         ssm_d[i_ssm], ssm_w_glu[i_ssm], ssm_b_glu[i_ssm], ssm_w_out[i_ssm])
            i_ssm += 1
        else:
            h = dsa_mixer(x, dsa_w_in[i_dsa], dsa_w_out[i_dsa])
            i_dsa += 1
        x = layer_norm(DN_ALPHA * x + h, ln_g[layer, 0], ln_b[layer, 0])
        h = hierarchical_moe(x, moe_rg_w[layer], moe_rg_b[layer], moe_re_w[layer], moe_re_b[layer],
                             moe_w_gate[layer], moe_w_up[layer], moe_w_down[layer])
        x = layer_norm(DN_ALPHA * x + h, ln_g[layer, 1], ln_b[layer, 1])
    return x
```

```python
import functools
import math

import jax
import jax.numpy as jnp
from jax import lax
from jax.experimental import pallas as pl
from jax.experimental.pallas import tpu as pltpu

F32 = jnp.float32
BF16 = jnp.bfloat16
I32 = jnp.int32

LANES = 128
VMEM_LIMIT = 56 * 1024 * 1024

LN_EPS = 1e-5
RMS_EPS = 1e-6
NEG_INF = -1e30

GDN_HEAD_DIM = 128
GDN_CONV = 4
GDN_CHUNK = 64
SSM_GROUP = 16
SSM_STATE = 64
SSM_CHUNK = 16
ATT_HEAD_DIM = 128
IDX_HEADS = 16
IDX_DIM = 128
TOPK_MAX = 256
Q_BLOCK = 128
MOE_GROUPS = 4
MOE_EPG = 8
MOE_BLOCK = 256


def _cparams(*sem):
    return pltpu.CompilerParams(dimension_semantics=sem, vmem_limit_bytes=VMEM_LIMIT)


def _tile(n, pref, mult=1):
    t = min(n, pref) // mult * mult
    while t > mult and n % t:
        t -= mult
    assert t > 0 and n % t == 0, (n, pref, mult)
    return t


def _dot(a, b):
    return jnp.dot(a, b, preferred_element_type=F32)


def _dot_nt(a, b):
    return lax.dot_general(a, b, (((1,), (1,)), ((), ())), preferred_element_type=F32)


def _dot_tn(a, b):
    return lax.dot_general(a, b, (((0,), (0,)), ((), ())), preferred_element_type=F32)


def _split_bf16(x):
    hi = x.astype(BF16)
    lo = (x - hi.astype(F32)).astype(BF16)
    return hi, lo


def _dot3(a, b):
    ah, al = _split_bf16(a)
    bh, bl = _split_bf16(b)
    return _dot(ah, bh) + (_dot(ah, bl) + _dot(al, bh))


def _mm_kernel(x_ref, w_ref, o_ref, wbf_ref):
    @pl.when(pl.program_id(1) == 0)
    def _():
        wbf_ref[...] = w_ref[...].astype(BF16)

    o_ref[...] = _dot(x_ref[...], wbf_ref[...]).astype(o_ref.dtype)


def _matmul(x, w, layer, n_cols, *, col0=0, out_dtype=F32, tm=1024, tn=512, name="matmul"):
    m, k = x.shape
    tm = _tile(m, tm)
    tn = _tile(n_cols, tn, LANES)
    assert col0 % tn == 0
    jb = col0 // tn
    return pl.pallas_call(
        _mm_kernel,
        out_shape=jax.ShapeDtypeStruct((m, n_cols), out_dtype),
        grid=(n_cols // tn, m // tm),
        in_specs=[pl.BlockSpec((tm, k), lambda j, i: (i, 0)),
                  pl.BlockSpec((None, k, tn), lambda j, i: (layer, 0, j + jb))],
        out_specs=pl.BlockSpec((tm, tn), lambda j, i: (i, j)),
        scratch_shapes=[pltpu.VMEM((k, tn), BF16)],
        compiler_params=_cparams("arbitrary", "arbitrary"),
        name=name,
    )(x, w)


def _layer_norm_rows(v, g, b):
    mu = jnp.mean(v, axis=-1, keepdims=True)
    c = v - mu
    var = jnp.mean(c * c, axis=-1, keepdims=True)
    return c * lax.rsqrt(var + LN_EPS) * g + b


def _write_ln_outputs(y, o_ref, obf_ref, otok_ref):
    o_ref[...] = y
    obf_ref[...] = y.astype(BF16)
    if otok_ref is not None:
        for c in range(otok_ref.shape[1]):
            otok_ref[:, c, :] = y[:, c * LANES:(c + 1) * LANES]


def _ln_mixer_kernel(alpha, x_ref, h_ref, g_ref, b_ref, o_ref, obf_ref, otok_ref):
    y = _layer_norm_rows(alpha * x_ref[...] + h_ref[...], g_ref[...], b_ref[...])
    _write_ln_outputs(y, o_ref, obf_ref, otok_ref)


def _ln_moe_kernel(alpha, x_ref, y_ref, route_ref, g_ref, b_ref, o_ref, obf_ref):
    nchunk = y_ref.shape[2]
    y0 = jnp.concatenate([y_ref[0, :, c, :] for c in range(nchunk)], axis=-1)
    y1 = jnp.concatenate([y_ref[1, :, c, :] for c in range(nchunk)], axis=-1)
    gate0 = route_ref[:, 2:3]
    gate1 = route_ref[:, 3:4]
    h = y0 * gate0 + y1 * gate1
    y = _layer_norm_rows(alpha * x_ref[...] + h, g_ref[...], b_ref[...])
    _write_ln_outputs(y, o_ref, obf_ref, None)


def _ln_mixer(x, h, ln_g, ln_b, layer, which, alpha, tm=512):
    n, d = x.shape
    tm = _tile(n, tm)
    row = pl.BlockSpec((tm, d), lambda i: (i, 0))
    par = pl.BlockSpec((None, 1, d), lambda i: (layer * 2 + which, 0, 0))
    return pl.pallas_call(
        functools.partial(_ln_mixer_kernel, alpha),
        out_shape=(jax.ShapeDtypeStruct((n, d), F32), jax.ShapeDtypeStruct((n, d), BF16),
                   jax.ShapeDtypeStruct((n, d // LANES, LANES), F32)),
        grid=(n // tm,),
        in_specs=[row, row, par, par],
        out_specs=(row, row, pl.BlockSpec((tm, d // LANES, LANES), lambda i: (i, 0, 0))),
        compiler_params=_cparams("parallel"),
        name="ln_mixer",
    )(x, h, ln_g.reshape(-1, 1, d), ln_b.reshape(-1, 1, d))


def _ln_moe(x, y2, route, ln_g, ln_b, layer, which, alpha, tm=512):
    n, d = x.shape
    tm = _tile(n, tm)
    row = pl.BlockSpec((tm, d), lambda i: (i, 0))
    par = pl.BlockSpec((None, 1, d), lambda i: (layer * 2 + which, 0, 0))
    return pl.pallas_call(
        functools.partial(_ln_moe_kernel, alpha),
        out_shape=(jax.ShapeDtypeStruct((n, d), F32), jax.ShapeDtypeStruct((n, d), BF16)),
        grid=(n // tm,),
        in_specs=[row, pl.BlockSpec((2, tm, d // LANES, LANES), lambda i: (0, i, 0, 0)),
                  pl.BlockSpec((tm, LANES), lambda i: (i, 0)), par, par],
        out_specs=(row, row),
        compiler_params=_cparams("parallel"),
        name="ln_moe",
    )(x, y2, route, ln_g.reshape(-1, 1, d), ln_b.reshape(-1, 1, d))


def _first_lane_where(cond, lane):
    return jnp.min(jnp.where(cond, lane, LANES), axis=-1, keepdims=True)


def _router_kernel(x_ref, w_ref, b_ref, o_ref):
    logits = _dot3(x_ref[...], w_ref[...]) + b_ref[...]
    lane = lax.broadcasted_iota(I32, logits.shape, 1)
    gmask = lane < MOE_GROUPS
    lg = jnp.where(gmask, logits, NEG_INF)
    eg = jnp.where(gmask, jnp.exp(lg - jnp.max(lg, axis=-1, keepdims=True)), 0.0)
    pg = eg / jnp.sum(eg, axis=-1, keepdims=True)
    p_g = jnp.max(pg, axis=-1, keepdims=True)
    g_idx = _first_lane_where(gmask & (pg == p_g), lane)
    lo = MOE_GROUPS + g_idx * MOE_EPG
    emask = (lane >= lo) & (lane < lo + MOE_EPG)
    le = jnp.where(emask, logits, NEG_INF)
    ee = jnp.where(emask, jnp.exp(le - jnp.max(le, axis=-1, keepdims=True)), 0.0)
    pe = jnp.where(emask, ee / jnp.sum(ee, axis=-1, keepdims=True), -1.0)
    p1 = jnp.max(pe, axis=-1, keepdims=True)
    i1 = _first_lane_where(pe == p1, lane)
    pe2 = jnp.where(lane == i1, -1.0, pe)
    p2 = jnp.max(pe2, axis=-1, keepdims=True)
    i2 = _first_lane_where(pe2 == p2, lane)
    den = p1 + p2
    e1 = (i1 - MOE_GROUPS).astype(F32)
    e2 = (i2 - MOE_GROUPS).astype(F32)
    out = jnp.where(lane == 0, e1, jnp.where(lane == 1, e2,
          jnp.where(lane == 2, p_g * p1 / den, jnp.where(lane == 3, p_g * p2 / den, 0.0))))
    o_ref[...] = out


def _router(x, rw, rb, tm=512):
    n, d = x.shape
    tm = _tile(n, tm)
    return pl.pallas_call(
        _router_kernel,
        out_shape=jax.ShapeDtypeStruct((n, LANES), F32),
        grid=(n // tm,),
        in_specs=[pl.BlockSpec((tm, d), lambda i: (i, 0)),
                  pl.BlockSpec((d, LANES), lambda i: (0, 0)),
                  pl.BlockSpec((1, LANES), lambda i: (0, 0))],
        out_specs=pl.BlockSpec((tm, LANES), lambda i: (i, 0)),
        compiler_params=_cparams("parallel"),
        name="moe_router",
    )(x, rw, rb)


def _expert_kernel(be_ref, nv_ref, tok_ref, dst_ref, nact_ref,
                   x_hbm, wg_ref, wu_ref, wd_ref, y_hbm,
                   xbuf, ybuf, wgb, wub, wdb, gsem, ssem):
    i = pl.program_id(0)
    nb = pl.num_programs(0)
    nact = nact_ref[0]
    blk = xbuf.shape[1]
    nchunk = xbuf.shape[2]
    slot = i % 2

    def gather_copy(tok, sl, r):
        return pltpu.make_async_copy(x_hbm.at[tok], xbuf.at[sl, r], gsem.at[sl])

    def scatter_copy(dst, sl, r):
        return pltpu.make_async_copy(ybuf.at[sl, r], y_hbm.at[dst], ssem.at[sl])

    def start_gather(b, sl):
        def body(r, c):
            gather_copy(tok_ref[b * blk + r], sl, r).start()
            return c
        lax.fori_loop(0, nv_ref[b], body, 0)

    def wait_gather(b, sl):
        def body(r, c):
            gather_copy(0, sl, r).wait()
            return c
        lax.fori_loop(0, nv_ref[b], body, 0)

    def start_scatter(b, sl):
        def body(r, c):
            scatter_copy(dst_ref[b * blk + r], sl, r).start()
            return c
        lax.fori_loop(0, nv_ref[b], body, 0)

    def wait_scatter(b, sl):
        def body(r, c):
            scatter_copy(0, sl, r).wait()
            return c
        lax.fori_loop(0, nv_ref[b], body, 0)

    @pl.when(i == 0)
    def _():
        xbuf[...] = jnp.zeros(xbuf.shape, xbuf.dtype)

        @pl.when(nact > 0)
        def _():
            start_gather(0, 0)

    @pl.when(i < nact)
    def _():
        wait_gather(i, slot)

        @pl.when(i + 1 < nact)
        def _():
            start_gather(i + 1, 1 - slot)

        prev = be_ref[jnp.maximum(i - 1, 0)]

        @pl.when((i == 0) | (be_ref[i] != prev))
        def _():
            wgb[...] = wg_ref[...].astype(BF16)
            wub[...] = wu_ref[...].astype(BF16)
            wdb[...] = wd_ref[...].astype(BF16)

        @pl.when(i >= 2)
        def _():
            wait_scatter(i - 2, slot)

        xb = jnp.concatenate([xbuf[slot, :, c, :] for c in range(nchunk)], axis=-1).astype(BF16)
        a = _dot(xb, wgb[...])
        u = _dot(xb, wub[...])
        h = (jax.nn.silu(a) * u).astype(BF16)
        y = _dot(h, wdb[...])
        for c in range(nchunk):
            ybuf[slot, :, c, :] = y[:, c * LANES:(c + 1) * LANES]
        start_scatter(i, slot)

    @pl.when(i == nb - 1)
    def _():
        for back in (2, 1):
            b = nact - back

            @pl.when(b >= 0)
            def _():
                wait_scatter(b, b % 2)


def _moe_dispatch(route, n_exp, blk):
    n = route.shape[0]
    n_asg = 2 * n
    flat_e = route[:, :2].astype(I32).reshape(-1)
    order = jnp.argsort(flat_e, stable=True).astype(I32)
    counts = jnp.sum(flat_e[:, None] == jnp.arange(n_exp, dtype=I32)[None, :], axis=0, dtype=I32)
    padded = (counts + blk - 1) // blk * blk
    pad_end = jnp.cumsum(padded)
    pad_start = pad_end - padded
    cnt_start = jnp.cumsum(counts) - counts
    n_blocks = (n_asg + n_exp * (blk - 1) + blk - 1) // blk
    block_start = jnp.arange(n_blocks, dtype=I32) * blk
    block_expert = jnp.minimum(jnp.searchsorted(pad_end, block_start, side='right'), n_exp - 1).astype(I32)
    block_nvalid = jnp.clip(counts[block_expert] - (block_start - pad_start[block_expert]), 0, blk).astype(I32)
    n_active = (pad_end[-1] // blk).astype(I32).reshape(1)
    slot = jnp.arange(n_blocks * blk, dtype=I32)
    se = block_expert[slot // blk]
    r = slot - pad_start[se]
    valid = (r >= 0) & (r < counts[se])
    asg = order[jnp.clip(cnt_start[se] + r, 0, n_asg - 1)]
    slot_tok = jnp.where(valid, asg // 2, 0).astype(I32)
    slot_dst = jnp.where(valid, (asg % 2) * n + asg // 2, 0).astype(I32)
    return block_expert, block_nvalid, slot_tok, slot_dst, n_active


def _moe_experts(x_tok, route, w_gate, w_up, w_down, layer):
    n, nchunk, _ = x_tok.shape
    d = nchunk * LANES
    n_exp, ff = w_gate.shape[1], w_gate.shape[3]
    blk = MOE_BLOCK
    be, nv, slot_tok, slot_dst, nact = _moe_dispatch(route, n_exp, blk)
    n_blocks = be.shape[0]
    grid_spec = pltpu.PrefetchScalarGridSpec(
        num_scalar_prefetch=5,
        grid=(n_blocks,),
        in_specs=[pl.BlockSpec(memory_space=pl.ANY),
                  pl.BlockSpec((None, None, d, ff), lambda i, be, *_: (layer, be[i], 0, 0)),
                  pl.BlockSpec((None, None, d, ff), lambda i, be, *_: (layer, be[i], 0, 0)),
                  pl.BlockSpec((None, None, ff, d), lambda i, be, *_: (layer, be[i], 0, 0))],
        out_specs=pl.BlockSpec(memory_space=pl.ANY),
        scratch_shapes=[pltpu.VMEM((2, blk, nchunk, LANES), F32),
                        pltpu.VMEM((2, blk, nchunk, LANES), F32),
                        pltpu.VMEM((d, ff), BF16), pltpu.VMEM((d, ff), BF16), pltpu.VMEM((ff, d), BF16),
                        pltpu.SemaphoreType.DMA((2,)), pltpu.SemaphoreType.DMA((2,))],
    )
    y = pl.pallas_call(
        _expert_kernel,
        out_shape=jax.ShapeDtypeStruct((2 * n, nchunk, LANES), F32),
        grid_spec=grid_spec,
        compiler_params=_cparams("arbitrary"),
        name="moe_experts",
    )(be, nv, slot_tok, slot_dst, nact, x_tok, w_gate, w_up, w_down)
    return y.reshape(2, n, nchunk, LANES)


def _moe_layer(x, x_tok, ln_g, ln_b, layer, alpha, rg_w, rg_b, re_w, re_b, w_gate, w_up, w_down):
    d = x.shape[1]
    n_r = MOE_GROUPS + MOE_GROUPS * MOE_EPG
    rw = jnp.zeros((d, LANES), F32).at[:, :MOE_GROUPS].set(rg_w[layer]).at[:, MOE_GROUPS:n_r].set(re_w[layer])
    rb = jnp.zeros((1, LANES), F32).at[0, :MOE_GROUPS].set(rg_b[layer]).at[0, MOE_GROUPS:n_r].set(re_b[layer])
    route = _router(x, rw, rb)
    y2 = _moe_experts(x_tok, route, w_gate, w_up, w_down, layer)
    return _ln_moe(x, y2, route, ln_g, ln_b, layer, 1, alpha)


def _gdn_gate_kernel(nh, ab_ref, alog_ref, dtb_ref, gcb_ref, gct_ref):
    ab = ab_ref[...]
    tt = ab.shape[0]
    lane = lax.broadcasted_iota(I32, ab.shape, 1)
    pos = lax.broadcasted_iota(I32, ab.shape, 0) % GDN_CHUNK
    g = -jnp.exp(alog_ref[...]) * jax.nn.softplus(ab + dtb_ref[...])
    s = 1
    while s < GDN_CHUNK:
        g = g + jnp.where(pos >= s, pltpu.roll(g, s, 0), 0.0)
        s *= 2
    out = jnp.where(lane < nh, g, jnp.where(lane < 2 * nh, jax.nn.sigmoid(ab), 0.0))
    gcb_ref[...] = out
    for j in range(tt // LANES):
        gct_ref[j] = out[j * LANES:(j + 1) * LANES, :].T


def _gdn_gates(ab, alog_p, dtb_p, nh, tt=512):
    n = ab.shape[0]
    tt = _tile(n, tt, LANES)
    return pl.pallas_call(
        functools.partial(_gdn_gate_kernel, nh),
        out_shape=(jax.ShapeDtypeStruct((n, LANES), F32),
                   jax.ShapeDtypeStruct((n // LANES, LANES, LANES), F32)),
        grid=(n // tt,),
        in_specs=[pl.BlockSpec((tt, LANES), lambda i: (i, 0)),
                  pl.BlockSpec((1, LANES), lambda i: (0, 0)),
                  pl.BlockSpec((1, LANES), lambda i: (0, 0))],
        out_specs=(pl.BlockSpec((tt, LANES), lambda i: (i, 0)),
                   pl.BlockSpec((tt // LANES, LANES, LANES), lambda i: (i, 0, 0))),
        compiler_params=_cparams("parallel"),
        name="gdn_gates",
    )(ab, alog_p, dtb_p)


def _gdn_conv_kernel(hpb, ncb, x_ref, w_ref, o_ref, carry_ref):
    t = pl.program_id(1)
    j = pl.program_id(2)
    x = x_ref[...]
    tt, tc = x.shape

    @pl.when(t == 0)
    def _():
        carry_ref[j] = jnp.zeros((8, tc), F32)

    prev = jnp.tile(carry_ref[j], (tt // 8, 1))
    carry_ref[j] = x[tt - 8:tt, :]
    row = lax.broadcasted_iota(I32, x.shape, 0)
    acc = x * w_ref[GDN_CONV - 1:GDN_CONV, :]
    for s in range(1, GDN_CONV):
        xs = jnp.where(row < s, pltpu.roll(prev, s, 0), pltpu.roll(x, s, 0))
        acc = acc + xs * w_ref[GDN_CONV - 1 - s:GDN_CONV - s, :]
    y = jax.nn.silu(acc)
    kind = j // (ncb // 3)
    post = jnp.where(kind == 0, GDN_HEAD_DIM ** -0.5, 1.0)
    for hh in range(hpb):
        yh = y[:, hh * LANES:(hh + 1) * LANES]
        inv = lax.rsqrt(jnp.sum(yh * yh, axis=-1, keepdims=True) + RMS_EPS)
        o_ref[hh] = yh * (jnp.where(kind == 2, 1.0, inv) * post)


def _gdn_conv(proj, conv_w, layer, bsz, seq, nh, tt=256):
    hpb = _tile(nh, 8)
    tc = hpb * LANES
    ncb = 3 * nh // hpb
    hb = nh // hpb
    tt = _tile(seq, tt, 8)
    nt = seq // tt
    return pl.pallas_call(
        functools.partial(_gdn_conv_kernel, hpb, ncb),
        out_shape=jax.ShapeDtypeStruct((3, bsz, nh, seq, LANES), F32),
        grid=(bsz, nt, ncb),
        in_specs=[pl.BlockSpec((tt, tc), lambda b, t, j: (b * nt + t, j)),
                  pl.BlockSpec((None, GDN_CONV, tc), lambda b, t, j: (layer, 0, j))],
        out_specs=pl.BlockSpec((None, None, hpb, tt, LANES), lambda b, t, j: (j // hb, b, j % hb, t, 0)),
        scratch_shapes=[pltpu.VMEM((ncb, 8, tc), F32)],
        compiler_params=_cparams("arbitrary", "arbitrary", "arbitrary"),
        name="gdn_conv",
    )(proj, conv_w)


def _head_column(tile, lane, idx):
    return jnp.sum(jnp.where(lane == idx, tile, 0.0), axis=-1, keepdims=True)


def _gdn_local_kernel(nh, q_ref, k_ref, gcb_ref, gct_ref, l_ref, a_ref):
    h = pl.program_id(1)
    c = GDN_CHUNK
    lane = lax.broadcasted_iota(I32, (LANES, LANES), 1)
    row = lax.broadcasted_iota(I32, (LANES, LANES), 0)
    lower = ((row // c) == (lane // c)) & (lane <= row)
    for s in range(q_ref.shape[0] // LANES):
        rows = slice(s * LANES, (s + 1) * LANES)
        q = q_ref[rows, :]
        k = k_ref[rows, :]
        gcb = gcb_ref[rows, :]
        gcol = _head_column(gcb, lane, h)
        bcol = _head_column(gcb, lane, nh + h)
        grow = gct_ref[s, pl.ds(h, 1), :]
        decay = jnp.where(lower, jnp.exp(jnp.where(lower, gcol - grow, 0.0)), 0.0)
        kbf = k.astype(BF16)
        kk = _dot_nt((k * bcol).astype(BF16), kbf)
        qk = _dot_nt(q.astype(BF16), kbf)
        lmat = jnp.where(lane < row, kk * decay, 0.0)
        amat = qk * decay
        for half in range(LANES // c):
            sl = slice(half * c, (half + 1) * c)
            l_ref[(LANES // c) * s + half] = lmat[sl, sl]
            a_ref[(LANES // c) * s + half] = amat[sl, sl].astype(BF16)


def _gdn_local(qkv, gcb, gct, bsz, seq, nh, tt=512):
    c = GDN_CHUNK
    tt = _tile(seq, tt, LANES)
    nt = seq // tt
    nc = seq // c
    qk_spec = lambda which: pl.BlockSpec((None, None, None, tt, LANES), lambda b, h, t: (which, b, h, t, 0))
    out_spec = pl.BlockSpec((None, None, tt // c, c, c), lambda b, h, t: (b, h, t, 0, 0))
    return pl.pallas_call(
        functools.partial(_gdn_local_kernel, nh),
        out_shape=(jax.ShapeDtypeStruct((bsz, nh, nc, c, c), F32),
                   jax.ShapeDtypeStruct((bsz, nh, nc, c, c), BF16)),
        grid=(bsz, nh, nt),
        in_specs=[qk_spec(0), qk_spec(1),
                  pl.BlockSpec((tt, LANES), lambda b, h, t: (b * nt + t, 0)),
                  pl.BlockSpec((tt // LANES, LANES, LANES), lambda b, h, t: (b * nt + t, 0, 0))],
        out_specs=(out_spec, out_spec),
        compiler_params=_cparams("parallel", "parallel", "parallel"),
        name="gdn_local",
    )(qkv, qkv, gcb, gct)


def _gdn_solve_kernel(l_ref, t_ref):
    c = l_ref.shape[0]
    sub = lax.broadcasted_iota(I32, (c, LANES), 0)
    t_ref[0] = jnp.where(sub == 0, 1.0, 0.0)

    def row_body(i, carry):
        def m_body(m, acc):
            return acc - l_ref[i, pl.ds(m, 1), :] * t_ref[m]
        t_ref[i] = lax.fori_loop(0, i, m_body, jnp.where(sub == i, 1.0, 0.0))
        return carry

    lax.fori_loop(1, c, row_body, 0)


def _gdn_solve(lp):
    c, _, p = lp.shape
    spec = pl.BlockSpec((c, c, LANES), lambda i: (0, 0, i))
    return pl.pallas_call(
        _gdn_solve_kernel,
        out_shape=jax.ShapeDtypeStruct(lp.shape, F32),
        grid=(p // LANES,),
        in_specs=[spec],
        out_specs=spec,
        compiler_params=_cparams("parallel"),
        name="gdn_solve",
    )(lp)


def _gdn_recur_kernel(nh, q_ref, k_ref, v_ref, z_ref, gcb_ref, tm_ref, at_ref, ng_ref, o_ref):
    h = pl.program_id(1)
    c = GDN_CHUNK
    dv = v_ref.shape[1]
    lane = lax.broadcasted_iota(I32, (c, LANES), 1)

    def body(ci, state):
        rows = pl.ds(pl.multiple_of(ci * c, c), c)
        q = q_ref[rows, :]
        k = k_ref[rows, :]
        v = v_ref[rows, :]
        gcb = gcb_ref[rows, :]
        gcol = _head_column(gcb, lane, h)
        bcol = _head_column(gcb, lane, nh + h)
        glast = gcol[c - 1:c, :]
        eg = jnp.exp(gcol)
        kb = k * bcol
        sol = _dot3(tm_ref[ci], jnp.concatenate([v * bcol, kb * eg], axis=-1))
        u = sol[:, :dv]
        w = sol[:, dv:]
        sb = state.astype(BF16)
        v_new = u - _dot(w.astype(BF16), sb)
        vnb = v_new.astype(BF16)
        o = _dot((q * eg).astype(BF16), sb) + _dot(at_ref[ci], vnb)
        kd = k * jnp.exp(glast - gcol)
        state = state * jnp.exp(glast) + _dot_tn(kd.astype(BF16), vnb)
        o = o * lax.rsqrt(jnp.mean(o * o, axis=-1, keepdims=True) + RMS_EPS) * ng_ref[...]
        o = o * jax.nn.silu(z_ref[rows, :])
        o_ref[rows, :] = o.astype(o_ref.dtype)
        return state

    lax.fori_loop(0, q_ref.shape[0] // c, body, jnp.zeros((k_ref.shape[1], dv), F32))


def _gdn_recur(qkv, proj, gcb, tm, at, norm_g, layer, bsz, seq, nh):
    c = GDN_CHUNK
    nc = seq // c
    head = lambda which: pl.BlockSpec((None, None, None, seq, LANES), lambda b, h: (which, b, h, 0, 0))
    mat = pl.BlockSpec((None, None, nc, c, c), lambda b, h: (b, h, 0, 0, 0))
    return pl.pallas_call(
        functools.partial(_gdn_recur_kernel, nh),
        out_shape=jax.ShapeDtypeStruct((bsz * seq, nh * LANES), BF16),
        grid=(bsz, nh),
        in_specs=[head(0), head(1), head(2),
                  pl.BlockSpec((seq, LANES), lambda b, h: (b, 3 * nh + h)),
                  pl.BlockSpec((seq, LANES), lambda b, h: (b, 0)),
                  mat, mat,
                  pl.BlockSpec((None, 1, LANES), lambda b, h: (layer, 0, 0))],
        out_specs=pl.BlockSpec((seq, LANES), lambda b, h: (b, h)),
        compiler_params=_cparams("parallel", "parallel"),
        name="gdn_recur",
    )(qkv, qkv, qkv, proj, gcb, tm, at, norm_g.reshape(-1, 1, LANES))


def _pad_lanes(v):
    return jnp.zeros((1, LANES), F32).at[0, :v.shape[0]].set(v)


def _gdn_mixer(x_bf, bsz, seq, w_in, conv_w, a_log, dt_bias, norm_g, w_out, layer):
    n, d = x_bf.shape
    nh = d // GDN_HEAD_DIM
    c = GDN_CHUNK
    proj = _matmul(x_bf, w_in, layer, 4 * d, name="gdn_in")
    w_ab = jnp.zeros((1, d, LANES), F32).at[0, :, :2 * nh].set(w_in[layer, :, 4 * d:])
    ab = _matmul(x_bf, w_ab, 0, LANES, name="gdn_in_ab")
    gcb, gct = _gdn_gates(ab, _pad_lanes(a_log[layer]), _pad_lanes(dt_bias[layer]), nh)
    qkv = _gdn_conv(proj, conv_w, layer, bsz, seq, nh)
    lc, at = _gdn_local(qkv, gcb, gct, bsz, seq, nh)
    p = bsz * nh * (seq // c)
    p_pad = -(-p // LANES) * LANES
    lp = jnp.transpose(lc.reshape(p, c, c), (1, 2, 0))
    lp = jnp.pad(lp, ((0, 0), (0, 0), (0, p_pad - p)))
    tp = _gdn_solve(lp)
    tm = jnp.transpose(tp[:, :, :p], (2, 0, 1)).reshape(lc.shape)
    og = _gdn_recur(qkv, proj, gcb, tm, at, norm_g, layer, bsz, seq, nh)
    return _matmul(og, w_out, layer, d, name="gdn_out")


def _s5_matrices(b_re, b_im, c_re, c_im, a_re, a_im, log_dt):
    lc = SSM_CHUNK
    g, p, nch = b_re.shape
    dt = jnp.exp(log_dt)[:, None]
    mag = jnp.exp(a_re * dt)
    ang = a_im * dt
    lb_re = mag * jnp.cos(ang)
    lb_im = mag * jnp.sin(ang)
    den = jnp.square(a_re) + jnp.square(a_im)
    f_re = ((lb_re - 1.0) * a_re + lb_im * a_im) / den
    f_im = (lb_im * a_re - (lb_re - 1.0) * a_im) / den
    bb_re = f_re[..., None] * b_re - f_im[..., None] * b_im
    bb_im = f_re[..., None] * b_im + f_im[..., None] * b_re
    dd = jnp.arange(lc + 1, dtype=F32)[:, None, None]
    pw_mag = jnp.exp(dd * (a_re * dt))
    pw_re = pw_mag * jnp.cos(dd * ang)
    pw_im = pw_mag * jnp.sin(dd * ang)
    cp_re = c_re[None] * pw_re[:, :, None, :] - c_im[None] * pw_im[:, :, None, :]
    cp_im = c_re[None] * pw_im[:, :, None, :] + c_im[None] * pw_re[:, :, None, :]
    hp = lax.Precision.HIGHEST
    kd = (jnp.einsum('dgcp,gpk->dgck', cp_re[:lc], bb_re, precision=hp)
          - jnp.einsum('dgcp,gpk->dgck', cp_im[:lc], bb_im, precision=hp))
    sig = jnp.arange(lc)[:, None]
    tau = jnp.arange(lc)[None, :]
    m5 = jnp.where((tau >= sig)[:, :, None, None, None], kd[jnp.clip(tau - sig, 0, lc - 1)], 0.0)
    m = jnp.transpose(m5, (2, 0, 4, 1, 3)).reshape(g, lc * nch, lc * nch)
    rev_re = pw_re[lc - 1::-1][:lc]
    rev_im = pw_im[lc - 1::-1][:lc]
    q_re = rev_re[..., None] * bb_re[None] - rev_im[..., None] * bb_im[None]
    q_im = rev_re[..., None] * bb_im[None] + rev_im[..., None] * bb_re[None]
    q_re = jnp.transpose(q_re, (1, 0, 3, 2)).reshape(g, lc * nch, p)
    q_im = jnp.transpose(q_im, (1, 0, 3, 2)).reshape(g, lc * nch, p)
    p_re = jnp.transpose(cp_re[1:], (1, 3, 0, 2)).reshape(g, p, lc * nch)
    p_im = -jnp.transpose(cp_im[1:], (1, 3, 0, 2)).reshape(g, p, lc * nch)
    a_pow = jnp.stack([pw_re[lc], pw_im[lc]], axis=1)
    return (m.astype(BF16), q_re.astype(BF16), q_im.astype(BF16),
            p_re.astype(BF16), p_im.astype(BF16), a_pow)


def _s5_scan_kernel(bsz, u_ref, m_ref, qre_ref, qim_ref, pre_ref, pim_ref, a_ref, y_ref,
                    xre, xim, hre, him):
    ub = u_ref[...].astype(BF16)
    xre[...] = _dot(ub, qre_ref[...])
    xim[...] = _dot(ub, qim_ref[...])
    ar = a_ref[0:1, :]
    ai = a_ref[1:2, :]

    def body(ci, carry):
        sr, si = carry
        rows = pl.ds(pl.multiple_of(ci * bsz, bsz), bsz)
        hre[rows, :] = sr
        him[rows, :] = si
        return ar * sr - ai * si + xre[rows, :], ar * si + ai * sr + xim[rows, :]

    zero = jnp.zeros((bsz, xre.shape[1]), F32)
    lax.fori_loop(0, u_ref.shape[0] // bsz, body, (zero, zero))
    y_ref[...] = (_dot(ub, m_ref[...]) + _dot(hre[...].astype(BF16), pre_ref[...])
                  + _dot(him[...].astype(BF16), pim_ref[...]))


def _s5_scan(uf, mats, bsz):
    g, r, w = uf.shape
    m, q_re, q_im, p_re, p_im, a_pow = mats
    p = q_re.shape[2]
    per_group = lambda a: pl.BlockSpec((None,) + a.shape[1:], lambda i: (i,) + (0,) * (a.ndim - 1))
    return pl.pallas_call(
        functools.partial(_s5_scan_kernel, bsz),
        out_shape=jax.ShapeDtypeStruct((g, r, w), F32),
        grid=(g,),
        in_specs=[per_group(a) for a in (uf, m, q_re, q_im, p_re, p_im, a_pow)],
        out_specs=per_group(uf),
        scratch_shapes=[pltpu.VMEM((r, p), F32)] * 4,
        compiler_params=_cparams("parallel"),
        name="s5_scan",
    )(uf, m, q_re, q_im, p_re, p_im, a_pow)


def _s5_gelu_kernel(y_ref, u_ref, d_ref, v_ref, vb_ref):
    v = jax.nn.gelu(y_ref[...] + d_ref[...] * u_ref[...])
    v_ref[...] = v
    vb_ref[...] = v.astype(BF16)


def _s5_gelu(y, u, d_skip, layer, tm=512):
    n, w = y.shape
    tm = _tile(n, tm)
    row = pl.BlockSpec((tm, w), lambda i: (i, 0))
    return pl.pallas_call(
        _s5_gelu_kernel,
        out_shape=(jax.ShapeDtypeStruct((n, w), F32), jax.ShapeDtypeStruct((n, w), BF16)),
        grid=(n // tm,),
        in_specs=[row, row, pl.BlockSpec((None, 1, w), lambda i: (layer, 0, 0))],
        out_specs=(row, row),
        compiler_params=_cparams("parallel"),
        name="s5_gelu",
    )(y, u, d_skip.reshape(-1, 1, w))


def _s5_glu_kernel(vb_ref, v_ref, w_ref, b_ref, o_ref, wbf_ref):
    @pl.when(pl.program_id(1) == 0)
    def _():
        wbf_ref[...] = w_ref[...].astype(BF16)

    z = _dot(vb_ref[...], wbf_ref[...]) + b_ref[...]
    o_ref[...] = (v_ref[...] * jax.nn.sigmoid(z)).astype(o_ref.dtype)


def _s5_glu(vb, v, w_glu, b_glu, layer, tm=1024, tn=512):
    n, w = v.shape
    tm = _tile(n, tm)
    tn = _tile(w, tn, LANES)
    return pl.pallas_call(
        _s5_glu_kernel,
        out_shape=jax.ShapeDtypeStruct((n, w), BF16),
        grid=(w // tn, n // tm),
        in_specs=[pl.BlockSpec((tm, w), lambda j, i: (i, 0)),
                  pl.BlockSpec((tm, tn), lambda j, i: (i, j)),
                  pl.BlockSpec((None, w, tn), lambda j, i: (layer, 0, j)),
                  pl.BlockSpec((None, 1, tn), lambda j, i: (layer, 0, j))],
        out_specs=pl.BlockSpec((tm, tn), lambda j, i: (i, j)),
        scratch_shapes=[pltpu.VMEM((w, tn), BF16)],
        compiler_params=_cparams("arbitrary", "arbitrary"),
        name="s5_glu",
    )(vb, v, w_glu, b_glu.reshape(-1, 1, w))


def _s5_mixer(x_bf, bsz, seq, w_in, b_re, b_im, c_re, c_im, a_re, a_im, log_dt, d_skip,
              w_glu, b_glu, w_out, layer):
    n, d = x_bf.shape
    w = w_in.shape[2]
    lc = SSM_CHUNK
    g = w // SSM_GROUP
    nck = seq // lc
    u = _matmul(x_bf, w_in, layer, w, name="s5_in")
    uf = jnp.transpose(u.reshape(bsz, nck, lc, g, SSM_GROUP), (3, 1, 0, 2, 4)).reshape(g, nck * bsz, lc * SSM_GROUP)
    mats = _s5_matrices(b_re[layer], b_im[layer], c_re[layer], c_im[layer], a_re[layer], a_im[layer], log_dt[layer])
    yf = _s5_scan(uf, mats, bsz)
    y = jnp.transpose(yf.reshape(g, nck, bsz, lc, SSM_GROUP), (2, 1, 3, 0, 4)).reshape(n, w)
    v, vb = _s5_gelu(y, u, d_skip, layer)
    hg = _s5_glu(vb, v, w_glu, b_glu, layer)
    return _matmul(hg, w_out, layer, d, name="s5_out")


INT_MIN = -2 ** 31


def _dsa_kernel(nsel, q_ref, qi_ref, wi_ref, k_ref, v_ref, ki_ref, o_ref,
                kbf, vbf, kibf, qis, qs, wb, keys, selm, m_s, l_s, acc_s):
    qb = pl.program_id(1)
    blk = Q_BLOCK
    nah = q_ref.shape[1] // ATT_HEAD_DIM
    nkb = qb + 1

    @pl.when(qb == 0)
    def _():
        kbf[...] = k_ref[...].astype(BF16)
        vbf[...] = v_ref[...].astype(BF16)
        kibf[...] = ki_ref[...].astype(BF16)

    qi = qi_ref[...]
    for h in range(IDX_HEADS):
        qis[h * blk:(h + 1) * blk, :] = qi[:, h * IDX_DIM:(h + 1) * IDX_DIM].astype(BF16)
        wb[h] = jnp.broadcast_to(wi_ref[:, h:h + 1] * (IDX_HEADS ** -0.5), (blk, LANES))
    q = q_ref[...]
    for h in range(nah):
        qs[h] = q[:, h * ATT_HEAD_DIM:(h + 1) * ATT_HEAD_DIM].astype(BF16)

    row = lax.broadcasted_iota(I32, (blk, blk), 0)
    col = lax.broadcasted_iota(I32, (blk, blk), 1)
    tpos = qb * blk + row

    def score_body(kb, carry):
        off = pl.multiple_of(kb * blk, blk)
        dots = _dot_nt(qis[...], kibf[pl.ds(off, blk), :])
        sc = jnp.zeros((blk, blk), F32)
        for h in range(IDX_HEADS):
            sc = sc + jnp.maximum(dots[h * blk:(h + 1) * blk, :] * (IDX_DIM ** -0.5), 0.0) * wb[h]
        sc = jnp.where(sc == 0.0, 0.0, sc)
        bits = pltpu.bitcast(sc, I32)
        key = jnp.where(bits < 0, bits ^ 0x7FFFFFFF, bits)
        keys[kb] = jnp.where(off + col <= tpos, key, INT_MIN)
        return carry

    lax.fori_loop(0, nkb, score_body, 0)

    def count_ge(cand):
        def body(kb, acc):
            return acc + jnp.where(keys[kb] >= cand, 1.0, 0.0)
        acc = lax.fori_loop(0, nkb, body, jnp.zeros((blk, blk), F32))
        return jnp.sum(acc, axis=-1, keepdims=True)

    def bit_body(t, th):
        cand = th + lax.shift_left(jnp.int32(1), 31 - t)
        return jnp.where(count_ge(cand) >= nsel, cand, th)

    th = lax.fori_loop(0, 32, bit_body, jnp.full((blk, 1), INT_MIN, I32))
    n_gt = count_ge(th + 1)
    n_ge = count_ge(th)
    overflow = jnp.where((n_ge > nsel) & (th > INT_MIN), 1.0, 0.0)
    has_ties = jnp.max(overflow) > 0.0

    @pl.when(jnp.logical_not(has_ties))
    def _():
        floor = jnp.maximum(th, INT_MIN + 1)

        def body(kb, carry):
            selm[kb] = jnp.where(keys[kb] >= floor, 1.0, 0.0)
            return carry
        lax.fori_loop(0, nkb, body, 0)

    @pl.when(has_ties)
    def _():
        need = nsel - n_gt
        tri = jnp.where(row <= col, 1.0, 0.0).astype(BF16)

        def body(kb, run):
            key = keys[kb]
            eq = jnp.where(key == th, 1.0, 0.0)
            rank = run + _dot(eq.astype(BF16), tri)
            take = (key > th) | ((key == th) & (rank <= need))
            selm[kb] = jnp.where(take & (key > INT_MIN), 1.0, 0.0)
            return run + jnp.sum(eq, axis=-1, keepdims=True)
        lax.fori_loop(0, nkb, body, jnp.zeros((blk, 1), F32))

    m_s[...] = jnp.full(m_s.shape, NEG_INF, F32)
    l_s[...] = jnp.zeros(l_s.shape, F32)
    acc_s[...] = jnp.zeros(acc_s.shape, F32)

    def att_body(kb, carry):
        off = pl.multiple_of(kb * blk, blk)
        kblk = kbf[pl.ds(off, blk), :]
        vblk = vbf[pl.ds(off, blk), :]
        sel = selm[kb] > 0.5
        dist = (tpos - (off + col)).astype(F32)
        for h in range(nah):
            slope = 2.0 ** (-8.0 * (h + 1) / nah)
            s = _dot_nt(qs[h], kblk) * (ATT_HEAD_DIM ** -0.5) - slope * dist
            s = jnp.where(sel, s, NEG_INF)
            m_old = m_s[h]
            m_new = jnp.maximum(m_old, jnp.max(s, axis=-1, keepdims=True))
            alpha = jnp.exp(m_old - m_new)
            p = jnp.where(sel, jnp.exp(s - m_new), 0.0)
            l_s[h] = alpha * l_s[h] + jnp.sum(p, axis=-1, keepdims=True)
            acc_s[h] = alpha * acc_s[h] + _dot(p.astype(BF16), vblk)
            m_s[h] = m_new
        return carry

    lax.fori_loop(0, nkb, att_body, 0)
    for h in range(nah):
        o_ref[:, h * ATT_HEAD_DIM:(h + 1) * ATT_HEAD_DIM] = (acc_s[h] / l_s[h]).astype(o_ref.dtype)


def _dsa_attention(proj, bsz, seq, d):
    blk = Q_BLOCK
    nq = seq // blk
    nah = d // ATT_HEAD_DIM
    iw = IDX_HEADS * IDX_DIM
    assert iw % d == 0
    kcol = (iw + d) // LANES
    nsel = float(min(TOPK_MAX, seq // 4))
    rowblk = lambda width, cidx: pl.BlockSpec((blk, width), lambda b, i: (b * nq + i, cidx))
    seqblk = lambda cidx: pl.BlockSpec((seq, LANES), lambda b, i: (b, cidx))
    return pl.pallas_call(
        functools.partial(_dsa_kernel, nsel),
        out_shape=jax.ShapeDtypeStruct((bsz * seq, d), BF16),
        grid=(bsz, nq),
        in_specs=[rowblk(d, iw // d), rowblk(iw, 0), rowblk(LANES, kcol + 3),
                  seqblk(kcol), seqblk(kcol + 1), seqblk(kcol + 2)],
        out_specs=pl.BlockSpec((blk, d), lambda b, i: (b * nq + i, 0)),
        scratch_shapes=[pltpu.VMEM((seq, LANES), BF16), pltpu.VMEM((seq, LANES), BF16),
                        pltpu.VMEM((seq, LANES), BF16),
                        pltpu.VMEM((IDX_HEADS * blk, IDX_DIM), BF16),
                        pltpu.VMEM((nah, blk, ATT_HEAD_DIM), BF16),
                        pltpu.VMEM((IDX_HEADS, blk, LANES), F32),
                        pltpu.VMEM((nq, blk, blk), I32),
                        pltpu.VMEM((nq, blk, blk), F32),
                        pltpu.VMEM((nah, blk, LANES), F32),
                        pltpu.VMEM((nah, blk, LANES), F32),
                        pltpu.VMEM((nah, blk, ATT_HEAD_DIM), F32)],
        compiler_params=_cparams("parallel", "arbitrary"),
        name="dsa_attention",
    )(proj, proj, proj, proj, proj, proj)


def _dsa_mixer(x_bf, bsz, seq, w_in, w_out, layer):
    n, d = x_bf.shape
    hd = ATT_HEAD_DIM
    iw = IDX_HEADS * IDX_DIM
    w = w_in[layer]
    o_k, o_v, o_qi, o_ki, o_wi = d, d + hd, d + 2 * hd, d + 2 * hd + iw, d + 2 * hd + iw + IDX_DIM
    cols = [w[:, o_qi:o_ki], w[:, :o_k], w[:, o_k:o_v], w[:, o_v:o_qi], w[:, o_ki:o_wi], w[:, o_wi:]]
    width = sum(c.shape[1] for c in cols)
    pad = -(-width // (4 * LANES)) * (4 * LANES) - width
    w_re = jnp.concatenate(cols + [jnp.zeros((d, pad), F32)], axis=1)[None]
    proj = _matmul(x_bf, w_re, 0, width + pad, name="dsa_in")
    o = _dsa_attention(proj, bsz, seq, d)
    return _matmul(o, w_out, layer, d, name="dsa_out")


def kernel(x, ln_g, ln_b, moe_rg_w, moe_rg_b, moe_re_w, moe_re_b, moe_w_gate, moe_w_up, moe_w_down,
           gdn_w_in, gdn_conv_w, gdn_a_log, gdn_dt_bias, gdn_norm_g, gdn_w_out,
           ssm_w_in, ssm_b_re, ssm_b_im, ssm_c_re, ssm_c_im, ssm_a_re, ssm_a_im, ssm_log_dt,
           ssm_d, ssm_w_glu, ssm_b_glu, ssm_w_out, dsa_w_in, dsa_w_out):
    bsz, seq, d = x.shape
    depth = ln_g.shape[0]
    alpha = (2.0 * depth) ** 0.25
    xf = x.reshape(bsz * seq, d)
    xb = xf.astype(BF16)
    counts = [0, 0, 0]
    for layer in range(depth):
        kind = layer % 3
        i = counts[kind]
        counts[kind] += 1
        if kind == 0:
            h = _gdn_mixer(xb, bsz, seq, gdn_w_in, gdn_conv_w, gdn_a_log, gdn_dt_bias, gdn_norm_g, gdn_w_out, i)
        elif kind == 1:
            h = _s5_mixer(xb, bsz, seq, ssm_w_in, ssm_b_re, ssm_b_im, ssm_c_re, ssm_c_im, ssm_a_re, ssm_a_im,
                          ssm_log_dt, ssm_d, ssm_w_glu, ssm_b_glu, ssm_w_out, i)
        else:
            h = _dsa_mixer(xb, bsz, seq, dsa_w_in, dsa_w_out, i)
        x1, _, x1_tok = _ln_mixer(xf, h, ln_g, ln_b, layer, 0, alpha)
        xf, xb = _moe_layer(x1, x1_tok, ln_g, ln_b, layer, alpha, moe_rg_w, moe_rg_b, moe_re_w, moe_re_b,
                            moe_w_gate, moe_w_up, moe_w_down)
    return xf.reshape(bsz, seq, d)
```

```python
import functools
import math

import jax
import jax.numpy as jnp
from jax import lax
from jax.experimental import pallas as pl
from jax.experimental.pallas import tpu as pltpu

F32 = jnp.float32
BF16 = jnp.bfloat16
I32 = jnp.int32

LANES = 128
VMEM_LIMIT = 56 * 1024 * 1024

LN_EPS = 1e-5
RMS_EPS = 1e-6
NEG_INF = -1e30

GDN_HEAD_DIM = 128
GDN_CONV = 4
GDN_CHUNK = 64
SSM_GROUP = 16
SSM_STATE = 64
SSM_CHUNK = 8
ATT_HEAD_DIM = 128
IDX_HEADS = 16
IDX_DIM = 128
TOPK_MAX = 256
Q_BLOCK = 128
MOE_GROUPS = 4
MOE_EPG = 8
MOE_BLOCK = 256


def _cparams(*sem):
    return pltpu.CompilerParams(dimension_semantics=sem, vmem_limit_bytes=VMEM_LIMIT)


def _tile(n, pref, mult=1):
    t = min(n, pref) // mult * mult
    while t > mult and n % t:
        t -= mult
    assert t > 0 and n % t == 0, (n, pref, mult)
    return t


def _dot(a, b):
    return jnp.dot(a, b, preferred_element_type=F32)


def _dot_nt(a, b):
    return lax.dot_general(a, b, (((1,), (1,)), ((), ())), preferred_element_type=F32)


def _dot_tn(a, b):
    return lax.dot_general(a, b, (((0,), (0,)), ((), ())), preferred_element_type=F32)


def _split_bf16(x):
    hi = x.astype(BF16)
    lo = (x - hi.astype(F32)).astype(BF16)
    return hi, lo


def _dot3(a, b):
    ah, al = _split_bf16(a)
    bh, bl = _split_bf16(b)
    return _dot(ah, bh) + (_dot(ah, bl) + _dot(al, bh))


def _mm_kernel(x_ref, w_ref, o_ref, wbf_ref):
    @pl.when(pl.program_id(1) == 0)
    def _():
        wbf_ref[...] = w_ref[...].astype(BF16)

    o_ref[...] = _dot(x_ref[...], wbf_ref[...]).astype(o_ref.dtype)


def _matmul(x, w, layer, n_cols, *, col0=0, out_dtype=F32, tm=1024, tn=512, name="matmul"):
    m, k = x.shape
    tm = _tile(m, tm)
    tn = _tile(n_cols, tn, LANES)
    assert col0 % tn == 0
    jb = col0 // tn
    return pl.pallas_call(
        _mm_kernel,
        out_shape=jax.ShapeDtypeStruct((m, n_cols), out_dtype),
        grid=(n_cols // tn, m // tm),
        in_specs=[pl.BlockSpec((tm, k), lambda j, i: (i, 0)),
                  pl.BlockSpec((None, k, tn), lambda j, i: (layer, 0, j + jb))],
        out_specs=pl.BlockSpec((tm, tn), lambda j, i: (i, j)),
        scratch_shapes=[pltpu.VMEM((k, tn), BF16)],
        compiler_params=_cparams("arbitrary", "arbitrary"),
        name=name,
    )(x, w)


def _layer_norm_rows(v, g, b):
    mu = jnp.mean(v, axis=-1, keepdims=True)
    c = v - mu
    var = jnp.mean(c * c, axis=-1, keepdims=True)
    return c * lax.rsqrt(var + LN_EPS) * g + b


def _ln_mixer_kernel(alpha, x_ref, h_ref, g_ref, b_ref, o_ref):
    o_ref[...] = _layer_norm_rows(alpha * x_ref[...] + h_ref[...], g_ref[...], b_ref[...])


def _ln_moe_kernel(alpha, x_ref, y_ref, route_ref, g_ref, b_ref, o_ref, obf_ref):
    h = y_ref[0] * route_ref[:, 2:3] + y_ref[1] * route_ref[:, 3:4]
    y = _layer_norm_rows(alpha * x_ref[...] + h, g_ref[...], b_ref[...])
    o_ref[...] = y
    obf_ref[...] = y.astype(BF16)


def _ln_mixer(x, h, ln_g, ln_b, layer, which, alpha, tm=512):
    n, d = x.shape
    tm = _tile(n, tm)
    row = pl.BlockSpec((tm, d), lambda i: (i, 0))
    par = pl.BlockSpec((None, 1, d), lambda i: (layer * 2 + which, 0, 0))
    return pl.pallas_call(
        functools.partial(_ln_mixer_kernel, alpha),
        out_shape=jax.ShapeDtypeStruct((n, d), F32),
        grid=(n // tm,),
        in_specs=[row, row, par, par],
        out_specs=row,
        compiler_params=_cparams("parallel"),
        name="ln_mixer",
    )(x, h, ln_g.reshape(-1, 1, d), ln_b.reshape(-1, 1, d))


def _ln_moe(x, y2, route, ln_g, ln_b, layer, which, alpha, tm=512):
    n, d = x.shape
    tm = _tile(n, tm)
    row = pl.BlockSpec((tm, d), lambda i: (i, 0))
    par = pl.BlockSpec((None, 1, d), lambda i: (layer * 2 + which, 0, 0))
    return pl.pallas_call(
        functools.partial(_ln_moe_kernel, alpha),
        out_shape=(jax.ShapeDtypeStruct((n, d), F32), jax.ShapeDtypeStruct((n, d), BF16)),
        grid=(n // tm,),
        in_specs=[row, pl.BlockSpec((2, tm, d), lambda i: (0, i, 0)),
                  pl.BlockSpec((tm, LANES), lambda i: (i, 0)), par, par],
        out_specs=(row, row),
        compiler_params=_cparams("parallel"),
        name="ln_moe",
    )(x, y2, route, ln_g.reshape(-1, 1, d), ln_b.reshape(-1, 1, d))


def _first_lane_where(cond, lane):
    return jnp.min(jnp.where(cond, lane, LANES), axis=-1, keepdims=True)


def _router_kernel(x_ref, w_ref, b_ref, o_ref):
    logits = _dot3(x_ref[...], w_ref[...]) + b_ref[...]
    lane = lax.broadcasted_iota(I32, logits.shape, 1)
    gmask = lane < MOE_GROUPS
    lg = jnp.where(gmask, logits, NEG_INF)
    eg = jnp.where(gmask, jnp.exp(lg - jnp.max(lg, axis=-1, keepdims=True)), 0.0)
    pg = eg / jnp.sum(eg, axis=-1, keepdims=True)
    p_g = jnp.max(pg, axis=-1, keepdims=True)
    g_idx = _first_lane_where(gmask & (pg == p_g), lane)
    lo = MOE_GROUPS + g_idx * MOE_EPG
    emask = (lane >= lo) & (lane < lo + MOE_EPG)
    le = jnp.where(emask, logits, NEG_INF)
    ee = jnp.where(emask, jnp.exp(le - jnp.max(le, axis=-1, keepdims=True)), 0.0)
    pe = jnp.where(emask, ee / jnp.sum(ee, axis=-1, keepdims=True), -1.0)
    p1 = jnp.max(pe, axis=-1, keepdims=True)
    i1 = _first_lane_where(pe == p1, lane)
    pe2 = jnp.where(lane == i1, -1.0, pe)
    p2 = jnp.max(pe2, axis=-1, keepdims=True)
    i2 = _first_lane_where(pe2 == p2, lane)
    den = p1 + p2
    e1 = (i1 - MOE_GROUPS).astype(F32)
    e2 = (i2 - MOE_GROUPS).astype(F32)
    out = jnp.where(lane == 0, e1, jnp.where(lane == 1, e2,
          jnp.where(lane == 2, p_g * p1 / den, jnp.where(lane == 3, p_g * p2 / den, 0.0))))
    o_ref[...] = out


def _router(x, rw, rb, tm=512):
    n, d = x.shape
    tm = _tile(n, tm)
    return pl.pallas_call(
        _router_kernel,
        out_shape=jax.ShapeDtypeStruct((n, LANES), F32),
        grid=(n // tm,),
        in_specs=[pl.BlockSpec((tm, d), lambda i: (i, 0)),
                  pl.BlockSpec((d, LANES), lambda i: (0, 0)),
                  pl.BlockSpec((1, LANES), lambda i: (0, 0))],
        out_specs=pl.BlockSpec((tm, LANES), lambda i: (i, 0)),
        compiler_params=_cparams("parallel"),
        name="moe_router",
    )(x, rw, rb)


def _expert_kernel(be_ref, nv_ref, tok_ref, dst_ref, nact_ref,
                   x_hbm, wg_ref, wu_ref, wd_ref, y_hbm,
                   xbuf, ybuf, wgb, wub, wdb, gsem, ssem):
    i = pl.program_id(0)
    nb = pl.num_programs(0)
    nact = nact_ref[0]
    blk = xbuf.shape[1]
    slot = i % 2
    group = 8

    def gather_copy(tok, sl, r, n=1):
        return pltpu.make_async_copy(x_hbm.at[pl.ds(tok, n), :], xbuf.at[sl, pl.ds(r, n), :], gsem.at[sl])

    def scatter_copy(dst, sl, r, n=1):
        return pltpu.make_async_copy(ybuf.at[sl, pl.ds(r, n), :], y_hbm.at[pl.ds(dst, n), :], ssem.at[sl])

    def for_rows(b, fn):
        n = nv_ref[b]
        full = n // group

        def grp(g, c):
            for j in range(group):
                fn(g * group + j)
            return c
        lax.fori_loop(0, full, grp, 0)

        def one(r, c):
            fn(r)
            return c
        lax.fori_loop(full * group, n, one, 0)

    def wait_rows(b, whole, row):
        @pl.when(nv_ref[b] == blk)
        def _():
            whole.wait()

        @pl.when(nv_ref[b] < blk)
        def _():
            for_rows(b, lambda r: row(r).wait())

    def start_gather(b, sl):
        for_rows(b, lambda r: gather_copy(tok_ref[b * blk + r], sl, r).start())

    def wait_gather(b, sl):
        wait_rows(b, gather_copy(0, sl, 0, blk), lambda r: gather_copy(0, sl, r))

    def start_scatter(b, sl):
        for_rows(b, lambda r: scatter_copy(dst_ref[b * blk + r], sl, r).start())

    def wait_scatter(b, sl):
        wait_rows(b, scatter_copy(0, sl, 0, blk), lambda r: scatter_copy(0, sl, r))

    @pl.when(i == 0)
    def _():
        xbuf[...] = jnp.zeros(xbuf.shape, xbuf.dtype)

        @pl.when(nact > 0)
        def _():
            start_gather(0, 0)

    @pl.when(i < nact)
    def _():
        wait_gather(i, slot)

        @pl.when(i + 1 < nact)
        def _():
            start_gather(i + 1, 1 - slot)

        prev = be_ref[jnp.maximum(i - 1, 0)]

        @pl.when((i == 0) | (be_ref[i] != prev))
        def _():
            wgb[...] = wg_ref[...].astype(BF16)
            wub[...] = wu_ref[...].astype(BF16)
            wdb[...] = wd_ref[...].astype(BF16)

        @pl.when(i >= 2)
        def _():
            wait_scatter(i - 2, slot)

        xb = xbuf[slot].astype(BF16)
        a = _dot(xb, wgb[...])
        u = _dot(xb, wub[...])
        h = (jax.nn.silu(a) * u).astype(BF16)
        ybuf[slot] = _dot(h, wdb[...])
        start_scatter(i, slot)

    @pl.when(i == nb - 1)
    def _():
        for back in (2, 1):
            b = nact - back

            @pl.when(b >= 0)
            def _():
                wait_scatter(b, b % 2)


def _moe_dispatch(route, n_exp, blk):
    n = route.shape[0]
    n_asg = 2 * n
    flat_e = route[:, :2].astype(I32).reshape(-1)
    order = jnp.argsort(flat_e, stable=True).astype(I32)
    counts = jnp.sum(flat_e[:, None] == jnp.arange(n_exp, dtype=I32)[None, :], axis=0, dtype=I32)
    padded = (counts + blk - 1) // blk * blk
    pad_end = jnp.cumsum(padded)
    pad_start = pad_end - padded
    cnt_start = jnp.cumsum(counts) - counts
    n_blocks = (n_asg + n_exp * (blk - 1) + blk - 1) // blk
    block_start = jnp.arange(n_blocks, dtype=I32) * blk
    block_expert = jnp.minimum(jnp.searchsorted(pad_end, block_start, side='right'), n_exp - 1).astype(I32)
    block_nvalid = jnp.clip(counts[block_expert] - (block_start - pad_start[block_expert]), 0, blk).astype(I32)
    n_active = (pad_end[-1] // blk).astype(I32).reshape(1)
    slot = jnp.arange(n_blocks * blk, dtype=I32)
    se = block_expert[slot // blk]
    r = slot - pad_start[se]
    valid = (r >= 0) & (r < counts[se])
    asg = order[jnp.clip(cnt_start[se] + r, 0, n_asg - 1)]
    slot_tok = jnp.where(valid, asg // 2, 0).astype(I32)
    slot_dst = jnp.where(valid, (asg % 2) * n + asg // 2, 0).astype(I32)
    return block_expert, block_nvalid, slot_tok, slot_dst, n_active


def _moe_experts(x, route, w_gate, w_up, w_down, layer):
    n, d = x.shape
    n_exp, ff = w_gate.shape[1], w_gate.shape[3]
    blk = MOE_BLOCK
    be, nv, slot_tok, slot_dst, nact = _moe_dispatch(route, n_exp, blk)
    n_blocks = be.shape[0]
    grid_spec = pltpu.PrefetchScalarGridSpec(
        num_scalar_prefetch=5,
        grid=(n_blocks,),
        in_specs=[pl.BlockSpec(memory_space=pl.ANY),
                  pl.BlockSpec((None, None, d, ff), lambda i, be, *_: (layer, be[i], 0, 0)),
                  pl.BlockSpec((None, None, d, ff), lambda i, be, *_: (layer, be[i], 0, 0)),
                  pl.BlockSpec((None, None, ff, d), lambda i, be, *_: (layer, be[i], 0, 0))],
        out_specs=pl.BlockSpec(memory_space=pl.ANY),
        scratch_shapes=[pltpu.VMEM((2, blk, d), F32),
                        pltpu.VMEM((2, blk, d), F32),
                        pltpu.VMEM((d, ff), BF16), pltpu.VMEM((d, ff), BF16), pltpu.VMEM((ff, d), BF16),
                        pltpu.SemaphoreType.DMA((2,)), pltpu.SemaphoreType.DMA((2,))],
    )
    y = pl.pallas_call(
        _expert_kernel,
        out_shape=jax.ShapeDtypeStruct((2 * n, d), F32),
        grid_spec=grid_spec,
        compiler_params=_cparams("arbitrary"),
        name="moe_experts",
    )(be, nv, slot_tok, slot_dst, nact, x, w_gate, w_up, w_down)
    return y.reshape(2, n, d)


def _moe_layer(x, ln_g, ln_b, layer, alpha, rg_w, rg_b, re_w, re_b, w_gate, w_up, w_down):
    d = x.shape[1]
    n_r = MOE_GROUPS + MOE_GROUPS * MOE_EPG
    rw = jnp.zeros((d, LANES), F32).at[:, :MOE_GROUPS].set(rg_w[layer]).at[:, MOE_GROUPS:n_r].set(re_w[layer])
    rb = jnp.zeros((1, LANES), F32).at[0, :MOE_GROUPS].set(rg_b[layer]).at[0, MOE_GROUPS:n_r].set(re_b[layer])
    route = _router(x, rw, rb)
    y2 = _moe_experts(x, route, w_gate, w_up, w_down, layer)
    return _ln_moe(x, y2, route, ln_g, ln_b, layer, 1, alpha)


def _gdn_gate_kernel(nh, ab_ref, alog_ref, dtb_ref, gcb_ref, gct_ref):
    ab = ab_ref[...]
    tt = ab.shape[0]
    lane = lax.broadcasted_iota(I32, ab.shape, 1)
    pos = lax.broadcasted_iota(I32, ab.shape, 0) % GDN_CHUNK
    g = -jnp.exp(alog_ref[...]) * jax.nn.softplus(ab + dtb_ref[...])
    s = 1
    while s < GDN_CHUNK:
        g = g + jnp.where(pos >= s, pltpu.roll(g, s, 0), 0.0)
        s *= 2
    out = jnp.where(lane < nh, g, jnp.where(lane < 2 * nh, jax.nn.sigmoid(ab), 0.0))
    gcb_ref[...] = out
    for j in range(tt // LANES):
        gct_ref[j] = out[j * LANES:(j + 1) * LANES, :].T


def _gdn_gates(ab, alog_p, dtb_p, nh, tt=512):
    n = ab.shape[0]
    tt = _tile(n, tt, LANES)
    return pl.pallas_call(
        functools.partial(_gdn_gate_kernel, nh),
        out_shape=(jax.ShapeDtypeStruct((n, LANES), F32),
                   jax.ShapeDtypeStruct((n // LANES, LANES, LANES), F32)),
        grid=(n // tt,),
        in_specs=[pl.BlockSpec((tt, LANES), lambda i: (i, 0)),
                  pl.BlockSpec((1, LANES), lambda i: (0, 0)),
                  pl.BlockSpec((1, LANES), lambda i: (0, 0))],
        out_specs=(pl.BlockSpec((tt, LANES), lambda i: (i, 0)),
                   pl.BlockSpec((tt // LANES, LANES, LANES), lambda i: (i, 0, 0))),
        compiler_params=_cparams("parallel"),
        name="gdn_gates",
    )(ab, alog_p, dtb_p)


def _gdn_conv_kernel(hpb, ncb, x_ref, w_ref, o_ref, carry_ref):
    t = pl.program_id(1)
    j = pl.program_id(2)
    x = x_ref[...]
    tt, tc = x.shape

    @pl.when(t == 0)
    def _():
        carry_ref[j] = jnp.zeros((8, tc), F32)

    prev = jnp.tile(carry_ref[j], (tt // 8, 1))
    carry_ref[j] = x[tt - 8:tt, :]
    row = lax.broadcasted_iota(I32, x.shape, 0)
    acc = x * w_ref[GDN_CONV - 1:GDN_CONV, :]
    for s in range(1, GDN_CONV):
        xs = jnp.where(row < s, pltpu.roll(prev, s, 0), pltpu.roll(x, s, 0))
        acc = acc + xs * w_ref[GDN_CONV - 1 - s:GDN_CONV - s, :]
    y = jax.nn.silu(acc)
    kind = j // (ncb // 3)
    post = jnp.where(kind == 0, GDN_HEAD_DIM ** -0.5, 1.0)
    for hh in range(hpb):
        yh = y[:, hh * LANES:(hh + 1) * LANES]
        inv = lax.rsqrt(jnp.sum(yh * yh, axis=-1, keepdims=True) + RMS_EPS)
        o_ref[hh] = yh * (jnp.where(kind == 2, 1.0, inv) * post)


def _gdn_conv(proj, conv_w, layer, bsz, seq, nh, tt=256):
    hpb = _tile(nh, 8)
    tc = hpb * LANES
    ncb = 3 * nh // hpb
    hb = nh // hpb
    tt = _tile(seq, tt, 8)
    nt = seq // tt
    return pl.pallas_call(
        functools.partial(_gdn_conv_kernel, hpb, ncb),
        out_shape=jax.ShapeDtypeStruct((3, bsz, nh, seq, LANES), F32),
        grid=(bsz, nt, ncb),
        in_specs=[pl.BlockSpec((tt, tc), lambda b, t, j: (b * nt + t, j)),
                  pl.BlockSpec((None, GDN_CONV, tc), lambda b, t, j: (layer, 0, j))],
        out_specs=pl.BlockSpec((None, None, hpb, tt, LANES), lambda b, t, j: (j // hb, b, j % hb, t, 0)),
        scratch_shapes=[pltpu.VMEM((ncb, 8, tc), F32)],
        compiler_params=_cparams("arbitrary", "arbitrary", "arbitrary"),
        name="gdn_conv",
    )(proj, conv_w)


def _head_column(tile, lane, idx):
    return jnp.sum(jnp.where(lane == idx, tile, 0.0), axis=-1, keepdims=True)


def _gdn_local_kernel(nh, q_ref, k_ref, gcb_ref, gct_ref, l_ref, a_ref):
    h = pl.program_id(1)
    c = GDN_CHUNK
    lane = lax.broadcasted_iota(I32, (LANES, LANES), 1)
    row = lax.broadcasted_iota(I32, (LANES, LANES), 0)
    lower = ((row // c) == (lane // c)) & (lane <= row)
    for s in range(q_ref.shape[0] // LANES):
        rows = slice(s * LANES, (s + 1) * LANES)
        q = q_ref[rows, :]
        k = k_ref[rows, :]
        gcb = gcb_ref[rows, :]
        gcol = _head_column(gcb, lane, h)
        bcol = _head_column(gcb, lane, nh + h)
        grow = gct_ref[s, pl.ds(h, 1), :]
        decay = jnp.where(lower, jnp.exp(jnp.where(lower, gcol - grow, 0.0)), 0.0)
        kbf = k.astype(BF16)
        kk = _dot_nt((k * bcol).astype(BF16), kbf)
        qk = _dot_nt(q.astype(BF16), kbf)
        lmat = jnp.where(lane < row, kk * decay, 0.0)
        amat = qk * decay
        for half in range(LANES // c):
            sl = slice(half * c, (half + 1) * c)
            l_ref[(LANES // c) * s + half] = lmat[sl, sl]
            a_ref[(LANES // c) * s + half] = amat[sl, sl].astype(BF16)


def _gdn_local(qkv, gcb, gct, bsz, seq, nh, tt=512):
    c = GDN_CHUNK
    tt = _tile(seq, tt, LANES)
    nt = seq // tt
    nc = seq // c
    qk_spec = lambda which: pl.BlockSpec((None, None, None, tt, LANES), lambda b, h, t: (which, b, h, t, 0))
    out_spec = pl.BlockSpec((None, None, tt // c, c, c), lambda b, h, t: (b, h, t, 0, 0))
    return pl.pallas_call(
        functools.partial(_gdn_local_kernel, nh),
        out_shape=(jax.ShapeDtypeStruct((bsz, nh, nc, c, c), F32),
                   jax.ShapeDtypeStruct((bsz, nh, nc, c, c), BF16)),
        grid=(bsz, nh, nt),
        in_specs=[qk_spec(0), qk_spec(1),
                  pl.BlockSpec((tt, LANES), lambda b, h, t: (b * nt + t, 0)),
                  pl.BlockSpec((tt // LANES, LANES, LANES), lambda b, h, t: (b * nt + t, 0, 0))],
        out_specs=(out_spec, out_spec),
        compiler_params=_cparams("parallel", "parallel", "parallel"),
        name="gdn_local",
    )(qkv, qkv, gcb, gct)


def _gdn_solve_kernel(l_ref, t_ref):
    c = l_ref.shape[0]
    sub = lax.broadcasted_iota(I32, (c, LANES), 0)
    t_ref[0] = jnp.where(sub == 0, 1.0, 0.0)

    def row_body(i, carry):
        def m_body(m, acc):
            return acc - l_ref[i, pl.ds(m, 1), :] * t_ref[m]
        t_ref[i] = lax.fori_loop(0, i, m_body, jnp.where(sub == i, 1.0, 0.0))
        return carry

    lax.fori_loop(1, c, row_body, 0)


def _gdn_solve(lp):
    c, _, p = lp.shape
    spec = pl.BlockSpec((c, c, LANES), lambda i: (0, 0, i))
    return pl.pallas_call(
        _gdn_solve_kernel,
        out_shape=jax.ShapeDtypeStruct(lp.shape, F32),
        grid=(p // LANES,),
        in_specs=[spec],
        out_specs=spec,
        compiler_params=_cparams("parallel"),
        name="gdn_solve",
    )(lp)


def _gdn_recur_kernel(nh, q_ref, k_ref, v_ref, z_ref, gcb_ref, tm_ref, at_ref, ng_ref, o_ref, st_ref):
    hblk = pl.program_id(1)
    c = GDN_CHUNK
    hb, tt, dv = v_ref.shape
    lane = lax.broadcasted_iota(I32, (c, LANES), 1)

    @pl.when(pl.program_id(2) == 0)
    def _():
        st_ref[...] = jnp.zeros(st_ref.shape, F32)

    def body(ci, carry):
        rows = pl.ds(pl.multiple_of(ci * c, c), c)
        gcb = gcb_ref[rows, :]
        for j in range(hb):
            h = hblk * hb + j
            q = q_ref[j, rows, :]
            k = k_ref[j, rows, :]
            v = v_ref[j, rows, :]
            gcol = _head_column(gcb, lane, h)
            bcol = _head_column(gcb, lane, nh + h)
            glast = gcol[c - 1:c, :]
            eg = jnp.exp(gcol)
            kb = k * bcol
            sol = _dot3(tm_ref[j, ci], jnp.concatenate([v * bcol, kb * eg], axis=-1))
            u = sol[:, :dv]
            w = sol[:, dv:]
            state = st_ref[j]
            sb = state.astype(BF16)
            v_new = u - _dot(w.astype(BF16), sb)
            vnb = v_new.astype(BF16)
            o = _dot((q * eg).astype(BF16), sb) + _dot(at_ref[j, ci], vnb)
            kd = k * jnp.exp(glast - gcol)
            st_ref[j] = state * jnp.exp(glast) + _dot_tn(kd.astype(BF16), vnb)
            o = o * lax.rsqrt(jnp.mean(o * o, axis=-1, keepdims=True) + RMS_EPS) * ng_ref[...]
            o = o * jax.nn.silu(z_ref[rows, j * dv:(j + 1) * dv])
            o_ref[rows, j * dv:(j + 1) * dv] = o.astype(o_ref.dtype)
        return carry

    lax.fori_loop(0, tt // c, body, 0)


def _gdn_recur(qkv, proj, gcb, tm, at, norm_g, layer, bsz, seq, nh, hb=8, tt=512):
    c = GDN_CHUNK
    hb = _tile(nh, hb)
    tt = _tile(seq, tt, c)
    nt = seq // tt
    head = lambda which: pl.BlockSpec((None, None, hb, tt, LANES), lambda b, h, t: (which, b, h, t, 0))
    mat = pl.BlockSpec((None, hb, tt // c, c, c), lambda b, h, t: (b, h, t, 0, 0))
    return pl.pallas_call(
        functools.partial(_gdn_recur_kernel, nh),
        out_shape=jax.ShapeDtypeStruct((bsz * seq, nh * LANES), BF16),
        grid=(bsz, nh // hb, nt),
        in_specs=[head(0), head(1), head(2),
                  pl.BlockSpec((tt, hb * LANES), lambda b, h, t: (b * nt + t, 3 * nh // hb + h)),
                  pl.BlockSpec((tt, LANES), lambda b, h, t: (b * nt + t, 0)),
                  mat, mat,
                  pl.BlockSpec((None, 1, LANES), lambda b, h, t: (layer, 0, 0))],
        out_specs=pl.BlockSpec((tt, hb * LANES), lambda b, h, t: (b * nt + t, h)),
        scratch_shapes=[pltpu.VMEM((hb, GDN_HEAD_DIM, LANES), F32)],
        compiler_params=_cparams("parallel", "parallel", "arbitrary"),
        name="gdn_recur",
    )(qkv, qkv, qkv, proj, gcb, tm, at, norm_g.reshape(-1, 1, LANES))


def _pad_lanes(v):
    return jnp.zeros((1, LANES), F32).at[0, :v.shape[0]].set(v)


def _gdn_mixer(x_bf, bsz, seq, w_in, conv_w, a_log, dt_bias, norm_g, w_out, layer):
    n, d = x_bf.shape
    nh = d // GDN_HEAD_DIM
    c = GDN_CHUNK
    proj = _matmul(x_bf, w_in, layer, 4 * d, tn=1024, name="gdn_in")
    w_ab = jnp.zeros((1, d, LANES), F32).at[0, :, :2 * nh].set(w_in[layer, :, 4 * d:])
    ab = _matmul(x_bf, w_ab, 0, LANES, name="gdn_in_ab")
    gcb, gct = _gdn_gates(ab, _pad_lanes(a_log[layer]), _pad_lanes(dt_bias[layer]), nh)
    qkv = _gdn_conv(proj, conv_w, layer, bsz, seq, nh)
    lc, at = _gdn_local(qkv, gcb, gct, bsz, seq, nh)
    p = bsz * nh * (seq // c)
    p_pad = -(-p // LANES) * LANES
    lp = jnp.transpose(lc.reshape(p, c, c), (1, 2, 0))
    lp = jnp.pad(lp, ((0, 0), (0, 0), (0, p_pad - p)))
    tp = _gdn_solve(lp)
    tm = jnp.transpose(tp[:, :, :p], (2, 0, 1)).reshape(lc.shape)
    og = _gdn_recur(qkv, proj, gcb, tm, at, norm_g, layer, bsz, seq, nh)
    return _matmul(og, w_out, layer, d, name="gdn_out")


def _s5_matrices(b_re, b_im, c_re, c_im, a_re, a_im, log_dt):
    lc = SSM_CHUNK
    g, p, nch = b_re.shape
    dt = jnp.exp(log_dt)[:, None]
    mag = jnp.exp(a_re * dt)
    ang = a_im * dt
    lb_re = mag * jnp.cos(ang)
    lb_im = mag * jnp.sin(ang)
    den = jnp.square(a_re) + jnp.square(a_im)
    f_re = ((lb_re - 1.0) * a_re + lb_im * a_im) / den
    f_im = (lb_im * a_re - (lb_re - 1.0) * a_im) / den
    bb_re = f_re[..., None] * b_re - f_im[..., None] * b_im
    bb_im = f_re[..., None] * b_im + f_im[..., None] * b_re
    dd = jnp.arange(lc + 1, dtype=F32)[:, None, None]
    pw_mag = jnp.exp(dd * (a_re * dt))
    pw_re = pw_mag * jnp.cos(dd * ang)
    pw_im = pw_mag * jnp.sin(dd * ang)
    cp_re = c_re[None] * pw_re[:, :, None, :] - c_im[None] * pw_im[:, :, None, :]
    cp_im = c_re[None] * pw_im[:, :, None, :] + c_im[None] * pw_re[:, :, None, :]
    hp = lax.Precision.HIGHEST
    kd = (jnp.einsum('dgcp,gpk->dgck', cp_re[:lc], bb_re, precision=hp)
          - jnp.einsum('dgcp,gpk->dgck', cp_im[:lc], bb_im, precision=hp))
    sig = jnp.arange(lc)[:, None]
    tau = jnp.arange(lc)[None, :]
    m5 = jnp.where((tau >= sig)[:, :, None, None, None], kd[jnp.clip(tau - sig, 0, lc - 1)], 0.0)
    rev_re = pw_re[lc - 1::-1][:lc]
    rev_im = pw_im[lc - 1::-1][:lc]
    q_re = rev_re[..., None] * bb_re[None] - rev_im[..., None] * bb_im[None]
    q_im = rev_re[..., None] * bb_im[None] + rev_im[..., None] * bb_re[None]
    m5, q_re, q_im, cp_re, cp_im = lax.optimization_barrier((m5, q_re, q_im, cp_re, cp_im))
    gpb = LANES // nch
    nb = g // gpb
    eye = jnp.eye(gpb, dtype=F32)
    width = lc * LANES
    m7 = jnp.transpose(m5.reshape(lc, lc, nb, gpb, nch, nch), (2, 0, 3, 5, 1, 4))
    mcat = m7[:, :, :, :, :, None, :] * eye[None, None, :, None, None, :, None]
    mcat = mcat.reshape(nb, width, width)

    def embed_q(q):
        q6 = jnp.transpose(q.reshape(lc, nb, gpb, p, nch), (1, 0, 2, 4, 3))
        return (q6[:, :, :, :, None, :] * eye[None, None, :, None, :, None]).reshape(nb, width, gpb * p)

    def embed_p(c):
        c6 = jnp.transpose(c.reshape(lc, nb, gpb, nch, p), (1, 2, 4, 0, 3))
        return (c6[:, :, :, :, None, :] * eye[None, :, None, None, :, None]).reshape(nb, gpb * p, width)

    qcat = jnp.concatenate([embed_q(q_re), embed_q(q_im)], axis=-1)
    pcat = jnp.concatenate([embed_p(cp_re[1:]), -embed_p(cp_im[1:])], axis=1)
    a_pow = jnp.stack([pw_re[lc].reshape(nb, gpb * p), pw_im[lc].reshape(nb, gpb * p)], axis=1)
    return mcat.astype(BF16), qcat.astype(BF16), pcat.astype(BF16), a_pow


def _s5_scan_kernel(lc, nck, u_ref, m_ref, q_ref, p_ref, a_ref, d_ref, v_ref, x_s, h_s):
    r = u_ref.shape[0] // lc
    bl = r // nck
    nhalf = x_s.shape[0] // 2
    step = lambda s: pl.ds(s, r, stride=lc)
    ucat = jnp.concatenate([u_ref[step(s), :] for s in range(lc)], axis=-1).astype(BF16)
    x = _dot(ucat, q_ref[...])
    for j in range(2 * nhalf):
        x_s[j] = x[:, j * LANES:(j + 1) * LANES]
    ar = [a_ref[0:1, j * LANES:(j + 1) * LANES] for j in range(nhalf)]
    ai = [a_ref[1:2, j * LANES:(j + 1) * LANES] for j in range(nhalf)]

    def body(ci, carry):
        rows = pl.ds(ci, bl, stride=nck)
        out = []
        for j in range(nhalf):
            sr, si = carry[j], carry[nhalf + j]
            h_s[j, rows, :] = sr
            h_s[nhalf + j, rows, :] = si
            out.append((ar[j] * sr - ai[j] * si + x_s[j, rows, :],
                        ar[j] * si + ai[j] * sr + x_s[nhalf + j, rows, :]))
        return tuple(o[0] for o in out) + tuple(o[1] for o in out)

    lax.fori_loop(0, nck, body, tuple(jnp.zeros((bl, LANES), F32) for _ in range(2 * nhalf)))
    h = jnp.concatenate([h_s[j] for j in range(2 * nhalf)], axis=-1).astype(BF16)
    y = _dot(ucat, m_ref[...]) + _dot(h, p_ref[...])
    for s in range(lc):
        v_ref[step(s), :] = jax.nn.gelu(y[:, s * LANES:(s + 1) * LANES] + d_ref[...] * u_ref[step(s), :])


def _s5_scan(u, mats, d_skip, layer, bsz, seq):
    n, w = u.shape
    lc = SSM_CHUNK
    nck = seq // lc
    mcat, qcat, pcat, a_pow = mats
    nb = mcat.shape[0]
    bl = _tile(bsz, 4)
    rows = bl * seq
    r = rows // lc
    per_block = lambda a: pl.BlockSpec((None,) + a.shape[1:], lambda i, j: (i,) + (0,) * (a.ndim - 1))
    io = pl.BlockSpec((rows, LANES), lambda i, j: (j, i))
    return pl.pallas_call(
        functools.partial(_s5_scan_kernel, lc, nck),
        out_shape=jax.ShapeDtypeStruct((n, w), F32),
        grid=(nb, bsz // bl),
        in_specs=[io, per_block(mcat), per_block(qcat), per_block(pcat), per_block(a_pow),
                  pl.BlockSpec((None, 1, LANES), lambda i, j: (layer, 0, i))],
        out_specs=io,
        scratch_shapes=[pltpu.VMEM((qcat.shape[2] // LANES, r, LANES), F32)] * 2,
        compiler_params=_cparams("parallel", "parallel"),
        name="s5_scan",
    )(u, mcat, qcat, pcat, a_pow, d_skip.reshape(-1, 1, w))


def _s5_glu_kernel(vrow_ref, v_ref, w_ref, b_ref, o_ref, wbf_ref):
    @pl.when(pl.program_id(1) == 0)
    def _():
        wbf_ref[...] = w_ref[...].astype(BF16)

    z = _dot(vrow_ref[...].astype(BF16), wbf_ref[...]) + b_ref[...]
    o_ref[...] = (v_ref[...] * jax.nn.sigmoid(z)).astype(o_ref.dtype)


def _s5_glu(v, w_glu, b_glu, layer, tm=1024, tn=512):
    n, w = v.shape
    tm = _tile(n, tm)
    tn = _tile(w, tn, LANES)
    return pl.pallas_call(
        _s5_glu_kernel,
        out_shape=jax.ShapeDtypeStruct((n, w), BF16),
        grid=(w // tn, n // tm),
        in_specs=[pl.BlockSpec((tm, w), lambda j, i: (i, 0)),
                  pl.BlockSpec((tm, tn), lambda j, i: (i, j)),
                  pl.BlockSpec((None, w, tn), lambda j, i: (layer, 0, j)),
                  pl.BlockSpec((None, 1, tn), lambda j, i: (layer, 0, j))],
        out_specs=pl.BlockSpec((tm, tn), lambda j, i: (i, j)),
        scratch_shapes=[pltpu.VMEM((w, tn), BF16)],
        compiler_params=_cparams("arbitrary", "arbitrary"),
        name="s5_glu",
    )(v, v, w_glu, b_glu.reshape(-1, 1, w))


def _s5_mixer(x_bf, bsz, seq, w_in, b_re, b_im, c_re, c_im, a_re, a_im, log_dt, d_skip,
              w_glu, b_glu, w_out, layer):
    n, d = x_bf.shape
    w = w_in.shape[2]
    u = _matmul(x_bf, w_in, layer, w, name="s5_in")
    mats = _s5_matrices(b_re[layer], b_im[layer], c_re[layer], c_im[layer], a_re[layer], a_im[layer], log_dt[layer])
    v = _s5_scan(u, mats, d_skip, layer, bsz, seq)
    hg = _s5_glu(v, w_glu, b_glu, layer)
    return _matmul(hg, w_out, layer, d, name="s5_out")


INT_MIN = -2 ** 31


def _dsa_kernel(nsel, q_ref, qi_ref, wi_ref, k_ref, v_ref, ki_ref, o_ref,
                kbf, vext, kibf, qis, qs, wb, slope_s, keys, selm, m_s, acc_s):
    qb = pl.program_id(1)
    blk = Q_BLOCK
    hd = ATT_HEAD_DIM
    nah = q_ref.shape[1] // hd
    nkb = qb + 1

    @pl.when(qb == 0)
    def _():
        kbf[...] = k_ref[...].astype(BF16)
        vext[:, :hd] = v_ref[...].astype(BF16)
        vext[:, hd:] = jnp.ones((vext.shape[0], vext.shape[1] - hd), BF16)
        kibf[...] = ki_ref[...].astype(BF16)
        for h in range(nah):
            slope_s[h * blk:(h + 1) * blk, :] = jnp.full((blk, LANES), 2.0 ** (-8.0 * (h + 1) / nah), F32)

    qi = qi_ref[...]
    for h in range(IDX_HEADS):
        qis[h * blk:(h + 1) * blk, :] = qi[:, h * IDX_DIM:(h + 1) * IDX_DIM].astype(BF16)
        wb[h] = jnp.broadcast_to(wi_ref[:, h:h + 1] * (IDX_HEADS ** -0.5 * IDX_DIM ** -0.5), (blk, LANES))
    q = q_ref[...]
    for h in range(nah):
        qs[h * blk:(h + 1) * blk, :] = q[:, h * hd:(h + 1) * hd].astype(BF16)

    row = lax.broadcasted_iota(I32, (blk, blk), 0)
    col = lax.broadcasted_iota(I32, (blk, blk), 1)
    tpos = qb * blk + row

    def score_body(kb, carry):
        off = pl.multiple_of(kb * blk, blk)
        dots = _dot_nt(qis[...], kibf[pl.ds(off, blk), :])
        sc = jnp.zeros((blk, blk), F32)
        for h in range(IDX_HEADS):
            sc = sc + jnp.maximum(dots[h * blk:(h + 1) * blk, :], 0.0) * wb[h]
        sc = jnp.where(sc == 0.0, 0.0, sc)
        bits = pltpu.bitcast(sc, I32)
        key = jnp.where(bits < 0, bits ^ 0x7FFFFFFF, bits)
        keys[kb] = jnp.where(off + col <= tpos, key, INT_MIN)
        return carry

    lax.fori_loop(0, nkb, score_body, 0)

    def count_ge(cands):
        def body(kb, accs):
            key = keys[kb]
            return tuple(acc + jnp.where(key >= cand, 1.0, 0.0) for acc, cand in zip(accs, cands))
        accs = lax.fori_loop(0, nkb, body, tuple(jnp.zeros((blk, blk), F32) for _ in cands))
        return [jnp.sum(acc, axis=-1, keepdims=True) for acc in accs]

    def bit_body(t, th):
        cand = th + jnp.left_shift(jnp.int32(1), 31 - t)
        return jnp.where(count_ge([cand])[0] >= nsel, cand, th)

    th = lax.fori_loop(0, 32, bit_body, jnp.full((blk, 1), INT_MIN, I32))
    n_gt, n_ge = count_ge([th + 1, th])
    overflow = jnp.where((n_ge > nsel) & (th > INT_MIN), 1.0, 0.0)
    has_ties = jnp.max(overflow) > 0.0

    @pl.when(jnp.logical_not(has_ties))
    def _():
        floor = jnp.maximum(th, INT_MIN + 1)

        def body(kb, carry):
            selm[kb] = jnp.where(keys[kb] >= floor, 1.0, 0.0)
            return carry
        lax.fori_loop(0, nkb, body, 0)

    @pl.when(has_ties)
    def _():
        need = nsel - n_gt
        tri = jnp.where(row <= col, 1.0, 0.0).astype(BF16)

        def body(kb, run):
            key = keys[kb]
            eq = jnp.where(key == th, 1.0, 0.0)
            rank = run + _dot(eq.astype(BF16), tri)
            take = (key > th) | ((key == th) & (rank <= need))
            selm[kb] = jnp.where(take & (key > INT_MIN), 1.0, 0.0)
            return run + jnp.sum(eq, axis=-1, keepdims=True)
        lax.fori_loop(0, nkb, body, jnp.zeros((blk, 1), F32))

    def logits(kb):
        off = pl.multiple_of(kb * blk, blk)
        kpos = jnp.tile((off + col).astype(F32), (nah, 1))
        keep = jnp.tile(selm[kb], (nah, 1)) > 0.5
        s = _dot_nt(qs[...], kbf[pl.ds(off, blk), :]) * (hd ** -0.5) + slope_s[...] * kpos
        return off, keep, s

    m_s[...] = jnp.full(m_s.shape, NEG_INF, F32)

    def max_body(kb, carry):
        _, keep, s = logits(kb)
        m_s[...] = jnp.maximum(m_s[...], jnp.where(keep, s, NEG_INF))
        return carry

    lax.fori_loop(0, nkb, max_body, 0)
    m_s[...] = jnp.broadcast_to(jnp.max(m_s[...], axis=-1, keepdims=True), m_s.shape)
    acc_s[...] = jnp.zeros(acc_s.shape, F32)

    def acc_body(kb, carry):
        off, keep, s = logits(kb)
        p = jnp.where(keep, jnp.exp(s - m_s[...]), 0.0)
        acc_s[...] += _dot(p.astype(BF16), vext[pl.ds(off, blk), :])
        return carry

    lax.fori_loop(0, nkb, acc_body, 0)
    for h in range(nah):
        rows = slice(h * blk, (h + 1) * blk)
        o_ref[:, h * hd:(h + 1) * hd] = (acc_s[rows, :hd] / acc_s[rows, hd:]).astype(o_ref.dtype)


def _dsa_attention(proj, bsz, seq, d):
    blk = Q_BLOCK
    nq = seq // blk
    nah = d // ATT_HEAD_DIM
    iw = IDX_HEADS * IDX_DIM
    assert iw % d == 0
    kcol = (iw + d) // LANES
    nsel = float(min(TOPK_MAX, seq // 4))
    rowblk = lambda width, cidx: pl.BlockSpec((blk, width), lambda b, i: (b * nq + i, cidx))
    seqblk = lambda cidx: pl.BlockSpec((seq, LANES), lambda b, i: (b, cidx))
    return pl.pallas_call(
        functools.partial(_dsa_kernel, nsel),
        out_shape=jax.ShapeDtypeStruct((bsz * seq, d), BF16),
        grid=(bsz, nq),
        in_specs=[rowblk(d, iw // d), rowblk(iw, 0), rowblk(LANES, kcol + 3),
                  seqblk(kcol), seqblk(kcol + 1), seqblk(kcol + 2)],
        out_specs=pl.BlockSpec((blk, d), lambda b, i: (b * nq + i, 0)),
        scratch_shapes=[pltpu.VMEM((seq, ATT_HEAD_DIM), BF16),
                        pltpu.VMEM((seq, 2 * ATT_HEAD_DIM), BF16),
                        pltpu.VMEM((seq, IDX_DIM), BF16),
                        pltpu.VMEM((IDX_HEADS * blk, IDX_DIM), BF16),
                        pltpu.VMEM((nah * blk, ATT_HEAD_DIM), BF16),
                        pltpu.VMEM((IDX_HEADS, blk, LANES), F32),
                        pltpu.VMEM((nah * blk, LANES), F32),
                        pltpu.VMEM((nq, blk, blk), I32),
                        pltpu.VMEM((nq, blk, blk), F32),
                        pltpu.VMEM((nah * blk, LANES), F32),
                        pltpu.VMEM((nah * blk, 2 * ATT_HEAD_DIM), F32)],
        compiler_params=_cparams("parallel", "arbitrary"),
        name="dsa_attention",
    )(proj, proj, proj, proj, proj, proj)


def _dsa_mixer(x_bf, bsz, seq, w_in, w_out, layer):
    n, d = x_bf.shape
    hd = ATT_HEAD_DIM
    iw = IDX_HEADS * IDX_DIM
    w = w_in[layer]
    o_k, o_v, o_qi, o_ki, o_wi = d, d + hd, d + 2 * hd, d + 2 * hd + iw, d + 2 * hd + iw + IDX_DIM
    cols = [w[:, o_qi:o_ki], w[:, :o_k], w[:, o_k:o_v], w[:, o_v:o_qi], w[:, o_ki:o_wi], w[:, o_wi:]]
    width = sum(c.shape[1] for c in cols)
    pad = -(-width // (4 * LANES)) * (4 * LANES) - width
    w_re = jnp.concatenate(cols + [jnp.zeros((d, pad), F32)], axis=1)[None]
    proj = _matmul(x_bf, w_re, 0, width + pad, name="dsa_in")
    o = _dsa_attention(proj, bsz, seq, d)
    return _matmul(o, w_out, layer, d, name="dsa_out")


def kernel(x, ln_g, ln_b, moe_rg_w, moe_rg_b, moe_re_w, moe_re_b, moe_w_gate, moe_w_up, moe_w_down,
           gdn_w_in, gdn_conv_w, gdn_a_log, gdn_dt_bias, gdn_norm_g, gdn_w_out,
           ssm_w_in, ssm_b_re, ssm_b_im, ssm_c_re, ssm_c_im, ssm_a_re, ssm_a_im, ssm_log_dt,
           ssm_d, ssm_w_glu, ssm_b_glu, ssm_w_out, dsa_w_in, dsa_w_out):
    bsz, seq, d = x.shape
    depth = ln_g.shape[0]
    alpha = (2.0 * depth) ** 0.25
    xf = x.reshape(bsz * seq, d)
    xb = xf.astype(BF16)
    counts = [0, 0, 0]
    for layer in range(depth):
        kind = layer % 3
        i = counts[kind]
        counts[kind] += 1
        if kind == 0:
            h = _gdn_mixer(xb, bsz, seq, gdn_w_in, gdn_conv_w, gdn_a_log, gdn_dt_bias, gdn_norm_g, gdn_w_out, i)
        elif kind == 1:
            h = _s5_mixer(xb, bsz, seq, ssm_w_in, ssm_b_re, ssm_b_im, ssm_c_re, ssm_c_im, ssm_a_re, ssm_a_im,
                          ssm_log_dt, ssm_d, ssm_w_glu, ssm_b_glu, ssm_w_out, i)
        else:
            h = _dsa_mixer(xb, bsz, seq, dsa_w_in, dsa_w_out, i)
        x1 = _ln_mixer(xf, h, ln_g, ln_b, layer, 0, alpha)
        xf, xb = _moe_layer(x1, ln_g, ln_b, layer, alpha, moe_rg_w, moe_rg_b, moe_re_w, moe_re_b,
                            moe_w_gate, moe_w_up, moe_w_down)
    return xf.reshape(bsz, seq, d)
```

```python
import functools
import math

import jax
import jax.numpy as jnp
from jax import lax
from jax.experimental import pallas as pl
from jax.experimental.pallas import tpu as pltpu

F32 = jnp.float32
BF16 = jnp.bfloat16
I32 = jnp.int32

LANES = 128
VMEM_LIMIT = 56 * 1024 * 1024

LN_EPS = 1e-5
RMS_EPS = 1e-6
NEG_INF = -1e30

GDN_HEAD_DIM = 128
GDN_CONV = 4
GDN_CHUNK = 64
SSM_GROUP = 16
SSM_STATE = 64
SSM_CHUNK = 8
ATT_HEAD_DIM = 128
IDX_HEADS = 16
IDX_DIM = 128
TOPK_MAX = 256
Q_BLOCK = 128
MOE_GROUPS = 4
MOE_EPG = 8
MOE_BLOCK = 256


def _cparams(*sem):
    return pltpu.CompilerParams(dimension_semantics=sem, vmem_limit_bytes=VMEM_LIMIT)


def _tile(n, pref, mult=1):
    t = min(n, pref) // mult * mult
    while t > mult and n % t:
        t -= mult
    assert t > 0 and n % t == 0, (n, pref, mult)
    return t


def _dot(a, b):
    return jnp.dot(a, b, preferred_element_type=F32)


def _dot_nt(a, b):
    return lax.dot_general(a, b, (((1,), (1,)), ((), ())), preferred_element_type=F32)


def _dot_tn(a, b):
    return lax.dot_general(a, b, (((0,), (0,)), ((), ())), preferred_element_type=F32)


def _split_bf16(x):
    hi = x.astype(BF16)
    lo = (x - hi.astype(F32)).astype(BF16)
    return hi, lo


def _dot3(a, b):
    ah, al = _split_bf16(a)
    bh, bl = _split_bf16(b)
    return _dot(ah, bh) + (_dot(ah, bl) + _dot(al, bh))


def _mm_kernel(x_ref, w_ref, o_ref, wbf_ref):
    @pl.when(pl.program_id(1) == 0)
    def _():
        wbf_ref[...] = w_ref[...].astype(BF16)

    o_ref[...] = _dot(x_ref[...], wbf_ref[...]).astype(o_ref.dtype)


def _matmul(x, w, layer, n_cols, *, col0=0, out_dtype=F32, tm=1024, tn=512, name="matmul"):
    m, k = x.shape
    tm = _tile(m, tm)
    tn = _tile(n_cols, tn, LANES)
    assert col0 % tn == 0
    jb = col0 // tn
    return pl.pallas_call(
        _mm_kernel,
        out_shape=jax.ShapeDtypeStruct((m, n_cols), out_dtype),
        grid=(n_cols // tn, m // tm),
        in_specs=[pl.BlockSpec((tm, k), lambda j, i: (i, 0)),
                  pl.BlockSpec((None, k, tn), lambda j, i: (layer, 0, j + jb))],
        out_specs=pl.BlockSpec((tm, tn), lambda j, i: (i, j)),
        scratch_shapes=[pltpu.VMEM((k, tn), BF16)],
        compiler_params=_cparams("arbitrary", "arbitrary"),
        name=name,
    )(x, w)


def _layer_norm_rows(v, g, b):
    mu = jnp.mean(v, axis=-1, keepdims=True)
    c = v - mu
    var = jnp.mean(c * c, axis=-1, keepdims=True)
    return c * lax.rsqrt(var + LN_EPS) * g + b


def _ln_mixer_kernel(alpha, x_ref, h_ref, g_ref, b_ref, o_ref):
    o_ref[...] = _layer_norm_rows(alpha * x_ref[...] + h_ref[...], g_ref[...], b_ref[...])


def _ln_moe_kernel(alpha, x_ref, y_ref, route_ref, g_ref, b_ref, o_ref, obf_ref):
    h = y_ref[0] * route_ref[:, 2:3] + y_ref[1] * route_ref[:, 3:4]
    y = _layer_norm_rows(alpha * x_ref[...] + h, g_ref[...], b_ref[...])
    o_ref[...] = y
    obf_ref[...] = y.astype(BF16)


def _ln_mixer(x, h, ln_g, ln_b, layer, which, alpha, tm=512):
    n, d = x.shape
    tm = _tile(n, tm)
    row = pl.BlockSpec((tm, d), lambda i: (i, 0))
    par = pl.BlockSpec((None, 1, d), lambda i: (layer * 2 + which, 0, 0))
    return pl.pallas_call(
        functools.partial(_ln_mixer_kernel, alpha),
        out_shape=jax.ShapeDtypeStruct((n, d), F32),
        grid=(n // tm,),
        in_specs=[row, row, par, par],
        out_specs=row,
        compiler_params=_cparams("parallel"),
        name="ln_mixer",
    )(x, h, ln_g.reshape(-1, 1, d), ln_b.reshape(-1, 1, d))


def _ln_moe(x, y2, route, ln_g, ln_b, layer, which, alpha, tm=512):
    n, d = x.shape
    tm = _tile(n, tm)
    row = pl.BlockSpec((tm, d), lambda i: (i, 0))
    par = pl.BlockSpec((None, 1, d), lambda i: (layer * 2 + which, 0, 0))
    return pl.pallas_call(
        functools.partial(_ln_moe_kernel, alpha),
        out_shape=(jax.ShapeDtypeStruct((n, d), F32), jax.ShapeDtypeStruct((n, d), BF16)),
        grid=(n // tm,),
        in_specs=[row, pl.BlockSpec((2, tm, d), lambda i: (0, i, 0)),
                  pl.BlockSpec((tm, LANES), lambda i: (i, 0)), par, par],
        out_specs=(row, row),
        compiler_params=_cparams("parallel"),
        name="ln_moe",
    )(x, y2, route, ln_g.reshape(-1, 1, d), ln_b.reshape(-1, 1, d))


def _first_lane_where(cond, lane):
    return jnp.min(jnp.where(cond, lane, LANES), axis=-1, keepdims=True)


def _router_kernel(x_ref, w_ref, b_ref, o_ref, cnt_ref):
    logits = _dot3(x_ref[...], w_ref[...]) + b_ref[...]
    lane = lax.broadcasted_iota(I32, logits.shape, 1)
    gmask = lane < MOE_GROUPS
    lg = jnp.where(gmask, logits, NEG_INF)
    eg = jnp.where(gmask, jnp.exp(lg - jnp.max(lg, axis=-1, keepdims=True)), 0.0)
    pg = eg / jnp.sum(eg, axis=-1, keepdims=True)
    p_g = jnp.max(pg, axis=-1, keepdims=True)
    g_idx = _first_lane_where(gmask & (pg == p_g), lane)
    lo = MOE_GROUPS + g_idx * MOE_EPG
    emask = (lane >= lo) & (lane < lo + MOE_EPG)
    le = jnp.where(emask, logits, NEG_INF)
    ee = jnp.where(emask, jnp.exp(le - jnp.max(le, axis=-1, keepdims=True)), 0.0)
    pe = jnp.where(emask, ee / jnp.sum(ee, axis=-1, keepdims=True), -1.0)
    p1 = jnp.max(pe, axis=-1, keepdims=True)
    i1 = _first_lane_where(pe == p1, lane)
    pe2 = jnp.where(lane == i1, -1.0, pe)
    p2 = jnp.max(pe2, axis=-1, keepdims=True)
    i2 = _first_lane_where(pe2 == p2, lane)
    den = p1 + p2
    e1 = (i1 - MOE_GROUPS).astype(F32)
    e2 = (i2 - MOE_GROUPS).astype(F32)
    out = jnp.where(lane == 0, e1, jnp.where(lane == 1, e2,
          jnp.where(lane == 2, p_g * p1 / den, jnp.where(lane == 3, p_g * p2 / den, 0.0))))
    o_ref[...] = out

    @pl.when(pl.program_id(0) == 0)
    def _():
        cnt_ref[...] = jnp.zeros(cnt_ref.shape, F32)

    hit = (lane == i1 - MOE_GROUPS) | (lane == i2 - MOE_GROUPS)
    cnt_ref[...] += jnp.sum(jnp.where(hit, 1.0, 0.0), axis=0, keepdims=True)


def _router(x, rw, rb, tm=512):
    n, d = x.shape
    tm = _tile(n, tm)
    return pl.pallas_call(
        _router_kernel,
        out_shape=(jax.ShapeDtypeStruct((n, LANES), F32), jax.ShapeDtypeStruct((1, LANES), F32)),
        grid=(n // tm,),
        in_specs=[pl.BlockSpec((tm, d), lambda i: (i, 0)),
                  pl.BlockSpec((d, LANES), lambda i: (0, 0)),
                  pl.BlockSpec((1, LANES), lambda i: (0, 0))],
        out_specs=(pl.BlockSpec((tm, LANES), lambda i: (i, 0)), pl.BlockSpec((1, LANES), lambda i: (0, 0))),
        compiler_params=_cparams("arbitrary"),
        name="moe_router",
    )(x, rw, rb)


def _plan_kernel(n_tok, route_ref, pstart_ref, tok_ref, dst_ref, run_s, dvm_s, dsm_s, sem):
    i = pl.program_id(0)
    tm = route_ref.shape[0]
    cap = tok_ref.shape[0]
    unroll = 8

    @pl.when(i == 0)
    def _():
        run_s[...] = jnp.zeros(run_s.shape, F32)

        def clear(g, c):
            for j in range(unroll):
                tok_ref[g * unroll + j] = 0
                dst_ref[g * unroll + j] = 0
            return c
        lax.fori_loop(0, cap // unroll, clear, 0)

    route = route_ref[...]
    lane = lax.broadcasted_iota(I32, route.shape, 1).astype(F32)
    oh0 = jnp.where(lane == route[:, 0:1], 1.0, 0.0)
    oh1 = jnp.where(lane == route[:, 1:2], 1.0, 0.0)
    both = oh0 + oh1
    r_i = lax.broadcasted_iota(I32, (tm, tm), 0)
    c_i = lax.broadcasted_iota(I32, (tm, tm), 1)
    before = _dot(jnp.where(c_i < r_i, 1.0, 0.0).astype(BF16), both.astype(BF16))
    base = pstart_ref[...] + run_s[...] + before
    d0 = jnp.sum(oh0 * base, axis=-1, keepdims=True)
    d1 = jnp.sum(oh1 * base, axis=-1, keepdims=True)
    run_s[...] += jnp.sum(both, axis=0, keepdims=True)
    lane_i = lax.broadcasted_iota(I32, route.shape, 1)
    dcols = jnp.where(lane_i == 0, d0, jnp.where(lane_i == 1, d1, 0.0))
    dvm_s[...] = dcols.T[0:8, :].astype(I32)
    cp = pltpu.make_async_copy(dvm_s, dsm_s, sem)
    cp.start()
    cp.wait()

    def place(g, c):
        for j in range(unroll):
            r = g * unroll + j
            tok = i * tm + r
            s0 = dsm_s[0, r]
            s1 = dsm_s[1, r]
            tok_ref[s0] = tok
            dst_ref[s0] = tok
            tok_ref[s1] = tok
            dst_ref[s1] = n_tok + tok
        return c
    lax.fori_loop(0, tm // unroll, place, 0)


def _moe_plan(route, pad_start, cap, tm=256):
    n = route.shape[0]
    tm = _tile(n, tm, LANES)
    smem_out = pl.BlockSpec(memory_space=pltpu.SMEM)
    return pl.pallas_call(
        functools.partial(_plan_kernel, n),
        out_shape=(jax.ShapeDtypeStruct((cap,), I32), jax.ShapeDtypeStruct((cap,), I32)),
        grid=(n // tm,),
        in_specs=[pl.BlockSpec((tm, LANES), lambda i: (i, 0)), pl.BlockSpec((1, LANES), lambda i: (0, 0))],
        out_specs=(smem_out, smem_out),
        scratch_shapes=[pltpu.VMEM((1, LANES), F32), pltpu.VMEM((8, tm), I32), pltpu.SMEM((8, tm), I32),
                        pltpu.SemaphoreType.DMA],
        compiler_params=_cparams("arbitrary"),
        name="moe_plan",
    )(route, pad_start)


def _expert_kernel(be_ref, nv_ref, tok_ref, dst_ref, nact_ref,
                   x_hbm, wg_ref, wu_ref, wd_ref, y_hbm,
                   xbuf, ybuf, wgb, wub, wdb, gsem, ssem):
    i = pl.program_id(0)
    nb = pl.num_programs(0)
    nact = nact_ref[0]
    blk = xbuf.shape[1]
    slot = i % 2
    group = 8

    def gather_copy(tok, sl, r, n=1):
        return pltpu.make_async_copy(x_hbm.at[pl.ds(tok, n), :], xbuf.at[sl, pl.ds(r, n), :], gsem.at[sl])

    def scatter_copy(dst, sl, r, n=1):
        return pltpu.make_async_copy(ybuf.at[sl, pl.ds(r, n), :], y_hbm.at[pl.ds(dst, n), :], ssem.at[sl])

    def for_rows(b, fn):
        n = nv_ref[b]
        full = n // group

        def grp(g, c):
            for j in range(group):
                fn(g * group + j)
            return c
        lax.fori_loop(0, full, grp, 0)

        def one(r, c):
            fn(r)
            return c
        lax.fori_loop(full * group, n, one, 0)

    def wait_rows(b, whole, row):
        @pl.when(nv_ref[b] == blk)
        def _():
            whole.wait()

        @pl.when(nv_ref[b] < blk)
        def _():
            for_rows(b, lambda r: row(r).wait())

    def start_gather(b, sl):
        for_rows(b, lambda r: gather_copy(tok_ref[b * blk + r], sl, r).start())

    def wait_gather(b, sl):
        wait_rows(b, gather_copy(0, sl, 0, blk), lambda r: gather_copy(0, sl, r))

    def start_scatter(b, sl):
        for_rows(b, lambda r: scatter_copy(dst_ref[b * blk + r], sl, r).start())

    def wait_scatter(b, sl):
        wait_rows(b, scatter_copy(0, sl, 0, blk), lambda r: scatter_copy(0, sl, r))

    @pl.when(i == 0)
    def _():
        xbuf[...] = jnp.zeros(xbuf.shape, xbuf.dtype)

        @pl.when(nact > 0)
        def _():
            start_gather(0, 0)

    @pl.when(i < nact)
    def _():
        wait_gather(i, slot)

        @pl.when(i + 1 < nact)
        def _():
            start_gather(i + 1, 1 - slot)

        prev = be_ref[jnp.maximum(i - 1, 0)]

        @pl.when((i == 0) | (be_ref[i] != prev))
        def _():
            wgb[...] = wg_ref[...].astype(BF16)
            wub[...] = wu_ref[...].astype(BF16)
            wdb[...] = wd_ref[...].astype(BF16)

        @pl.when(i >= 2)
        def _():
            wait_scatter(i - 2, slot)

        xb = xbuf[slot].astype(BF16)
        a = _dot(xb, wgb[...])
        u = _dot(xb, wub[...])
        h = (jax.nn.silu(a) * u).astype(BF16)
        ybuf[slot] = _dot(h, wdb[...])
        start_scatter(i, slot)

    @pl.when(i == nb - 1)
    def _():
        for back in (2, 1):
            b = nact - back

            @pl.when(b >= 0)
            def _():
                wait_scatter(b, b % 2)


def _moe_dispatch(route, counts_row, n_exp, blk):
    n = route.shape[0]
    n_asg = 2 * n
    experts = jnp.arange(n_exp, dtype=I32)
    counts = counts_row[0, :n_exp].astype(I32)
    padded = (counts + blk - 1) // blk * blk
    pad_end = jnp.cumsum(padded)
    pad_start = pad_end - padded
    n_blocks = (n_asg + n_exp * (blk - 1) + blk - 1) // blk
    block_start = jnp.arange(n_blocks, dtype=I32) * blk
    block_expert = jnp.minimum(jnp.sum(pad_end[None, :] <= block_start[:, None], axis=1, dtype=I32), n_exp - 1)
    mine = block_expert[:, None] == experts[None, :]
    pick = lambda v: jnp.sum(jnp.where(mine, v[None, :], 0), axis=1, dtype=I32)
    block_nvalid = jnp.clip(pick(counts) - (block_start - pick(pad_start)), 0, blk).astype(I32)
    n_active = (pad_end[-1] // blk).astype(I32).reshape(1)
    pstart_row = jnp.zeros((1, LANES), F32).at[0, :n_exp].set(pad_start.astype(F32))
    slot_tok, slot_dst = _moe_plan(route, pstart_row, n_blocks * blk)
    return block_expert, block_nvalid, slot_tok, slot_dst, n_active


def _moe_experts(x, route, counts_row, w_gate, w_up, w_down, layer):
    n, d = x.shape
    n_exp, ff = w_gate.shape[1], w_gate.shape[3]
    blk = MOE_BLOCK
    be, nv, slot_tok, slot_dst, nact = _moe_dispatch(route, counts_row, n_exp, blk)
    n_blocks = be.shape[0]
    grid_spec = pltpu.PrefetchScalarGridSpec(
        num_scalar_prefetch=5,
        grid=(n_blocks,),
        in_specs=[pl.BlockSpec(memory_space=pl.ANY),
                  pl.BlockSpec((None, None, d, ff), lambda i, be, *_: (layer, be[i], 0, 0)),
                  pl.BlockSpec((None, None, d, ff), lambda i, be, *_: (layer, be[i], 0, 0)),
                  pl.BlockSpec((None, None, ff, d), lambda i, be, *_: (layer, be[i], 0, 0))],
        out_specs=pl.BlockSpec(memory_space=pl.ANY),
        scratch_shapes=[pltpu.VMEM((2, blk, d), F32),
                        pltpu.VMEM((2, blk, d), F32),
                        pltpu.VMEM((d, ff), BF16), pltpu.VMEM((d, ff), BF16), pltpu.VMEM((ff, d), BF16),
                        pltpu.SemaphoreType.DMA((2,)), pltpu.SemaphoreType.DMA((2,))],
    )
    y = pl.pallas_call(
        _expert_kernel,
        out_shape=jax.ShapeDtypeStruct((2 * n, d), F32),
        grid_spec=grid_spec,
        compiler_params=_cparams("arbitrary"),
        name="moe_experts",
    )(be, nv, slot_tok, slot_dst, nact, x, w_gate, w_up, w_down)
    return y.reshape(2, n, d)


def _moe_layer(x, ln_g, ln_b, layer, alpha, rg_w, rg_b, re_w, re_b, w_gate, w_up, w_down):
    d = x.shape[1]
    n_r = MOE_GROUPS + MOE_GROUPS * MOE_EPG
    rw = jnp.zeros((d, LANES), F32).at[:, :MOE_GROUPS].set(rg_w[layer]).at[:, MOE_GROUPS:n_r].set(re_w[layer])
    rb = jnp.zeros((1, LANES), F32).at[0, :MOE_GROUPS].set(rg_b[layer]).at[0, MOE_GROUPS:n_r].set(re_b[layer])
    route, counts_row = _router(x, rw, rb)
    y2 = _moe_experts(x, route, counts_row, w_gate, w_up, w_down, layer)
    return _ln_moe(x, y2, route, ln_g, ln_b, layer, 1, alpha)


def _gdn_gate_kernel(nh, ab_ref, alog_ref, dtb_ref, gcb_ref, gct_ref):
    ab = ab_ref[...]
    tt = ab.shape[0]
    lane = lax.broadcasted_iota(I32, ab.shape, 1)
    pos = lax.broadcasted_iota(I32, ab.shape, 0) % GDN_CHUNK
    g = -jnp.exp(alog_ref[...]) * jax.nn.softplus(ab + dtb_ref[...])
    s = 1
    while s < GDN_CHUNK:
        g = g + jnp.where(pos >= s, pltpu.roll(g, s, 0), 0.0)
        s *= 2
    out = jnp.where(lane < nh, g, jnp.where(lane < 2 * nh, jax.nn.sigmoid(ab), 0.0))
    gcb_ref[...] = out
    for j in range(tt // LANES):
        gct_ref[j] = out[j * LANES:(j + 1) * LANES, :].T


def _gdn_gates(ab, alog_p, dtb_p, nh, tt=512):
    n = ab.shape[0]
    tt = _tile(n, tt, LANES)
    return pl.pallas_call(
        functools.partial(_gdn_gate_kernel, nh),
        out_shape=(jax.ShapeDtypeStruct((n, LANES), F32),
                   jax.ShapeDtypeStruct((n // LANES, LANES, LANES), F32)),
        grid=(n // tt,),
        in_specs=[pl.BlockSpec((tt, LANES), lambda i: (i, 0)),
                  pl.BlockSpec((1, LANES), lambda i: (0, 0)),
                  pl.BlockSpec((1, LANES), lambda i: (0, 0))],
        out_specs=(pl.BlockSpec((tt, LANES), lambda i: (i, 0)),
                   pl.BlockSpec((tt // LANES, LANES, LANES), lambda i: (i, 0, 0))),
        compiler_params=_cparams("parallel"),
        name="gdn_gates",
    )(ab, alog_p, dtb_p)


def _gdn_conv_kernel(hpb, ncb, x_ref, w_ref, o_ref, carry_ref):
    t = pl.program_id(1)
    j = pl.program_id(2)
    x = x_ref[...]
    tt, tc = x.shape

    @pl.when(t == 0)
    def _():
        carry_ref[j] = jnp.zeros((8, tc), F32)

    prev = jnp.tile(carry_ref[j], (tt // 8, 1))
    carry_ref[j] = x[tt - 8:tt, :]
    row = lax.broadcasted_iota(I32, x.shape, 0)
    acc = x * w_ref[GDN_CONV - 1:GDN_CONV, :]
    for s in range(1, GDN_CONV):
        xs = jnp.where(row < s, pltpu.roll(prev, s, 0), pltpu.roll(x, s, 0))
        acc = acc + xs * w_ref[GDN_CONV - 1 - s:GDN_CONV - s, :]
    y = jax.nn.silu(acc)
    kind = j // (ncb // 3)
    post = jnp.where(kind == 0, GDN_HEAD_DIM ** -0.5, 1.0)
    for hh in range(hpb):
        yh = y[:, hh * LANES:(hh + 1) * LANES]
        inv = lax.rsqrt(jnp.sum(yh * yh, axis=-1, keepdims=True) + RMS_EPS)
        o_ref[hh] = yh * (jnp.where(kind == 2, 1.0, inv) * post)


def _gdn_conv(proj, conv_w, layer, bsz, seq, nh, tt=256):
    hpb = _tile(nh, 8)
    tc = hpb * LANES
    ncb = 3 * nh // hpb
    hb = nh // hpb
    tt = _tile(seq, tt, 8)
    nt = seq // tt
    return pl.pallas_call(
        functools.partial(_gdn_conv_kernel, hpb, ncb),
        out_shape=jax.ShapeDtypeStruct((3, bsz, nh, seq, LANES), F32),
        grid=(bsz, nt, ncb),
        in_specs=[pl.BlockSpec((tt, tc), lambda b, t, j: (b * nt + t, j)),
                  pl.BlockSpec((None, GDN_CONV, tc), lambda b, t, j: (layer, 0, j))],
        out_specs=pl.BlockSpec((None, None, hpb, tt, LANES), lambda b, t, j: (j // hb, b, j % hb, t, 0)),
        scratch_shapes=[pltpu.VMEM((ncb, 8, tc), F32)],
        compiler_params=_cparams("arbitrary", "arbitrary", "arbitrary"),
        name="gdn_conv",
    )(proj, conv_w)


def _head_column(tile, lane, idx):
    return jnp.sum(jnp.where(lane == idx, tile, 0.0), axis=-1, keepdims=True)


def _gdn_local_kernel(nh, q_ref, k_ref, gcb_ref, gct_ref, l_ref, a_ref):
    h = pl.program_id(1)
    c = GDN_CHUNK
    lane = lax.broadcasted_iota(I32, (LANES, LANES), 1)
    row = lax.broadcasted_iota(I32, (LANES, LANES), 0)
    lower = ((row // c) == (lane // c)) & (lane <= row)
    for s in range(q_ref.shape[0] // LANES):
        rows = slice(s * LANES, (s + 1) * LANES)
        q = q_ref[rows, :]
        k = k_ref[rows, :]
        gcb = gcb_ref[rows, :]
        gcol = _head_column(gcb, lane, h)
        bcol = _head_column(gcb, lane, nh + h)
        grow = gct_ref[s, pl.ds(h, 1), :]
        decay = jnp.where(lower, jnp.exp(jnp.where(lower, gcol - grow, 0.0)), 0.0)
        kbf = k.astype(BF16)
        kk = _dot_nt((k * bcol).astype(BF16), kbf)
        qk = _dot_nt(q.astype(BF16), kbf)
        l_ref[s] = jnp.where(lane < row, kk * decay, 0.0)
        a_ref[s] = (qk * decay).astype(BF16)


def _gdn_local(qkv, gcb, gct, bsz, seq, nh, tt=512):
    c = LANES
    tt = _tile(seq, tt, LANES)
    nt = seq // tt
    nc = seq // c
    qk_spec = lambda which: pl.BlockSpec((None, None, None, tt, LANES), lambda b, h, t: (which, b, h, t, 0))
    out_spec = pl.BlockSpec((None, None, tt // c, c, c), lambda b, h, t: (b, h, t, 0, 0))
    return pl.pallas_call(
        functools.partial(_gdn_local_kernel, nh),
        out_shape=(jax.ShapeDtypeStruct((bsz, nh, nc, c, c), F32),
                   jax.ShapeDtypeStruct((bsz, nh, nc, c, c), BF16)),
        grid=(bsz, nh, nt),
        in_specs=[qk_spec(0), qk_spec(1),
                  pl.BlockSpec((tt, LANES), lambda b, h, t: (b * nt + t, 0)),
                  pl.BlockSpec((tt // LANES, LANES, LANES), lambda b, h, t: (b * nt + t, 0, 0))],
        out_specs=(out_spec, out_spec),
        compiler_params=_cparams("parallel", "parallel", "parallel"),
        name="gdn_local",
    )(qkv, qkv, gcb, gct)


def _gdn_solve_kernel(l_ref, t_ref, lt_s, tt_s):
    c = GDN_CHUNK
    pr = 2 * c
    npb = l_ref.shape[0] // pr
    for i in range(c):
        both = l_ref[pl.ds(i, npb, stride=pr), :] + l_ref[pl.ds(c + i, npb, stride=pr), :]
        bt = both.T
        lt_s[0, i] = bt[:c, :]
        lt_s[1, i] = bt[c:, :]

    sub = lax.broadcasted_iota(I32, (c, npb), 0)
    unit = lambda i: jnp.where(sub == i, 1.0, 0.0)
    tt_s[0, 0] = unit(0)
    tt_s[1, 0] = unit(0)

    def row_body(i, carry):
        def term(g, m):
            return lt_s[g, i, pl.ds(m, 1), :] * tt_s[g, m]

        def m_body(mp, acc):
            m = 2 * mp
            return (acc[0] - term(0, m) - term(0, m + 1), acc[1] - term(1, m) - term(1, m + 1))
        acc = lax.fori_loop(0, i // 2, m_body, (unit(i), unit(i)))
        odd = jnp.where(i % 2 == 1, 1.0, 0.0)
        tt_s[0, i] = acc[0] - odd * term(0, i - 1)
        tt_s[1, i] = acc[1] - odd * term(1, i - 1)
        return carry

    lax.fori_loop(1, c, row_body, 0)

    lane = lax.broadcasted_iota(I32, (npb, pr), 1)
    for i in range(c):
        rows = jnp.concatenate([tt_s[0, i], tt_s[1, i]], axis=0).T
        t_ref[pl.ds(i, npb, stride=pr), :] = jnp.where(lane < c, rows, 0.0)
        t_ref[pl.ds(c + i, npb, stride=pr), :] = jnp.where(lane >= c, rows, 0.0)


def _gdn_solve(lpair, pairs_per_step=128):
    c = GDN_CHUNK
    pr = 2 * c
    n_pairs = lpair.shape[0] // pr
    npb = _tile(n_pairs, pairs_per_step)
    spec = pl.BlockSpec((npb * pr, pr), lambda i: (i, 0))
    return pl.pallas_call(
        _gdn_solve_kernel,
        out_shape=jax.ShapeDtypeStruct(lpair.shape, F32),
        grid=(n_pairs // npb,),
        in_specs=[spec],
        out_specs=spec,
        scratch_shapes=[pltpu.VMEM((2, c, c, npb), F32), pltpu.VMEM((2, c, c, npb), F32)],
        compiler_params=_cparams("parallel"),
        name="gdn_solve",
    )(lpair)


def _gdn_recur_kernel(nh, q_ref, k_ref, v_ref, z_ref, gcb_ref, tm_ref, at_ref, ng_ref, o_ref, st_ref):
    hblk = pl.program_id(1)
    c = GDN_CHUNK
    pr = 2 * c
    hb, tt, dv = v_ref.shape
    lane = lax.broadcasted_iota(I32, (c, LANES), 1)

    @pl.when(pl.program_id(2) == 0)
    def _():
        st_ref[...] = jnp.zeros(st_ref.shape, F32)

    def body(pi, carry):
        for half in range(2):
            rows = pl.ds(pl.multiple_of(pi * pr, pr) + half * c, c)
            blk = slice(half * c, (half + 1) * c)
            gcb = gcb_ref[rows, :]
            for j in range(hb):
                h = hblk * hb + j
                q = q_ref[j, rows, :]
                k = k_ref[j, rows, :]
                v = v_ref[j, rows, :]
                gcol = _head_column(gcb, lane, h)
                bcol = _head_column(gcb, lane, nh + h)
                glast = gcol[c - 1:c, :]
                eg = jnp.exp(gcol)
                kb = k * bcol
                sol = _dot3(tm_ref[j, pi, blk, blk], jnp.concatenate([v * bcol, kb * eg], axis=-1))
                u = sol[:, :dv]
                w = sol[:, dv:]
                state = st_ref[j]
                sb = state.astype(BF16)
                v_new = u - _dot(w.astype(BF16), sb)
                vnb = v_new.astype(BF16)
                o = _dot((q * eg).astype(BF16), sb) + _dot(at_ref[j, pi, blk, blk], vnb)
                kd = k * jnp.exp(glast - gcol)
                st_ref[j] = state * jnp.exp(glast) + _dot_tn(kd.astype(BF16), vnb)
                o = o * lax.rsqrt(jnp.mean(o * o, axis=-1, keepdims=True) + RMS_EPS) * ng_ref[...]
                o = o * jax.nn.silu(z_ref[rows, j * dv:(j + 1) * dv])
                o_ref[rows, j * dv:(j + 1) * dv] = o.astype(o_ref.dtype)
        return carry

    lax.fori_loop(0, tt // pr, body, 0)


def _gdn_recur(qkv, proj, gcb, tm, at, norm_g, layer, bsz, seq, nh, hb=8, tt=512):
    c = 2 * GDN_CHUNK
    hb = _tile(nh, hb)
    tt = _tile(seq, tt, c)
    nt = seq // tt
    head = lambda which: pl.BlockSpec((None, None, hb, tt, LANES), lambda b, h, t: (which, b, h, t, 0))
    mat = pl.BlockSpec((None, hb, tt // c, c, c), lambda b, h, t: (b, h, t, 0, 0))
    return pl.pallas_call(
        functools.partial(_gdn_recur_kernel, nh),
        out_shape=jax.ShapeDtypeStruct((bsz * seq, nh * LANES), BF16),
        grid=(bsz, nh // hb, nt),
        in_specs=[head(0), head(1), head(2),
                  pl.BlockSpec((tt, hb * LANES), lambda b, h, t: (b * nt + t, 3 * nh // hb + h)),
                  pl.BlockSpec((tt, LANES), lambda b, h, t: (b * nt + t, 0)),
                  mat, mat,
                  pl.BlockSpec((None, 1, LANES), lambda b, h, t: (layer, 0, 0))],
        out_specs=pl.BlockSpec((tt, hb * LANES), lambda b, h, t: (b * nt + t, h)),
        scratch_shapes=[pltpu.VMEM((hb, GDN_HEAD_DIM, LANES), F32)],
        compiler_params=_cparams("parallel", "parallel", "arbitrary"),
        name="gdn_recur",
    )(qkv, qkv, qkv, proj, gcb, tm, at, norm_g.reshape(-1, 1, LANES))


def _pad_lanes(v):
    return jnp.zeros((1, LANES), F32).at[0, :v.shape[0]].set(v)


def _gdn_mixer(x_bf, bsz, seq, w_in, conv_w, a_log, dt_bias, norm_g, w_out, layer):
    n, d = x_bf.shape
    nh = d // GDN_HEAD_DIM
    c = GDN_CHUNK
    proj = _matmul(x_bf, w_in, layer, 4 * d, tn=1024, name="gdn_in")
    w_ab = jnp.zeros((1, d, LANES), F32).at[0, :, :2 * nh].set(w_in[layer, :, 4 * d:])
    ab = _matmul(x_bf, w_ab, 0, LANES, name="gdn_in_ab")
    gcb, gct = _gdn_gates(ab, _pad_lanes(a_log[layer]), _pad_lanes(dt_bias[layer]), nh)
    qkv = _gdn_conv(proj, conv_w, layer, bsz, seq, nh)
    lpair, at = _gdn_local(qkv, gcb, gct, bsz, seq, nh)
    tm = _gdn_solve(lpair.reshape(-1, 2 * c)).reshape(lpair.shape)
    og = _gdn_recur(qkv, proj, gcb, tm, at, norm_g, layer, bsz, seq, nh)
    return _matmul(og, w_out, layer, d, name="gdn_out")


def _s5_matrices(b_re, b_im, c_re, c_im, a_re, a_im, log_dt):
    lc = SSM_CHUNK
    g, p, nch = b_re.shape
    dt = jnp.exp(log_dt)[:, None]
    mag = jnp.exp(a_re * dt)
    ang = a_im * dt
    lb_re = mag * jnp.cos(ang)
    lb_im = mag * jnp.sin(ang)
    den = jnp.square(a_re) + jnp.square(a_im)
    f_re = ((lb_re - 1.0) * a_re + lb_im * a_im) / den
    f_im = (lb_im * a_re - (lb_re - 1.0) * a_im) / den
    bb_re = f_re[..., None] * b_re - f_im[..., None] * b_im
    bb_im = f_re[..., None] * b_im + f_im[..., None] * b_re
    dd = jnp.arange(lc + 1, dtype=F32)[:, None, None]
    pw_mag = jnp.exp(dd * (a_re * dt))
    pw_re = pw_mag * jnp.cos(dd * ang)
    pw_im = pw_mag * jnp.sin(dd * ang)
    gpb = LANES // nch
    nb = g // gpb
    wide = gpb * p
    rows = lambda a: a.reshape(nb, LANES, p)
    lanes = lambda a: jnp.transpose(a.reshape(lc + 1, nb, wide), (1, 0, 2))
    bt_re = rows(jnp.transpose(bb_re, (0, 2, 1)))
    bt_im = rows(jnp.transpose(bb_im, (0, 2, 1)))
    pw_re, pw_im = lanes(pw_re), lanes(pw_im)
    width = lc * LANES
    per_block = lambda a: pl.BlockSpec((None,) + a.shape[1:], lambda i: (i, 0, 0))
    ins = (rows(c_re), rows(c_im), bt_re, bt_im, pw_re, pw_im)
    out = jax.ShapeDtypeStruct((nb, width, width), BF16)
    assert 2 * wide == width
    mcat, qcat, pcat = pl.pallas_call(
        functools.partial(_s5_operator_kernel, lc),
        out_shape=(out, out, out),
        grid=(nb,),
        in_specs=[per_block(a) for a in ins],
        out_specs=(per_block(out), per_block(out), per_block(out)),
        compiler_params=_cparams("parallel"),
        name="s5_operators",
    )(*ins)
    a_pow = jnp.stack([pw_re[:, lc], pw_im[:, lc]], axis=1)
    return mcat, qcat, pcat, a_pow


def _dot3_nt(a, b):
    ah, al = _split_bf16(a)
    bh, bl = _split_bf16(b)
    return _dot_nt(ah, bh) + (_dot_nt(ah, bl) + _dot_nt(al, bh))


def _s5_operator_kernel(lc, c_re_ref, c_im_ref, b_re_ref, b_im_ref, pw_re_ref, pw_im_ref, m_ref, q_ref, p_ref):
    p = c_re_ref.shape[1]
    wide = pw_re_ref.shape[1]
    row = lax.broadcasted_iota(I32, (LANES, wide), 0)
    lane = lax.broadcasted_iota(I32, (LANES, wide), 1)
    own = (row // SSM_GROUP) == (lane // p)

    def embed(ref):
        return jnp.where(own, jnp.concatenate([ref[...]] * (wide // p), axis=1), 0.0)

    c_re, c_im, b_re, b_im = embed(c_re_ref), embed(c_im_ref), embed(b_re_ref), embed(b_im_ref)
    taps = []
    for d in range(lc + 1):
        pr = pw_re_ref[d:d + 1, :]
        pi = pw_im_ref[d:d + 1, :]
        cp_re = c_re * pr - c_im * pi
        cp_im = c_re * pi + c_im * pr
        if d < lc:
            taps.append(_dot3_nt(b_re, cp_re) - _dot3_nt(b_im, cp_im))
            s = lc - 1 - d
            q_ref[s * LANES:(s + 1) * LANES, :wide] = (b_re * pr - b_im * pi).astype(BF16)
            q_ref[s * LANES:(s + 1) * LANES, wide:] = (b_re * pi + b_im * pr).astype(BF16)
        if d > 0:
            t = d - 1
            p_ref[:wide, t * LANES:(t + 1) * LANES] = cp_re.T.astype(BF16)
            p_ref[wide:, t * LANES:(t + 1) * LANES] = (-cp_im).T.astype(BF16)
    zero = jnp.zeros((LANES, LANES), BF16)
    for s in range(lc):
        for t in range(lc):
            m_ref[s * LANES:(s + 1) * LANES, t * LANES:(t + 1) * LANES] = (
                taps[t - s].astype(BF16) if t >= s else zero)


def _s5_scan_kernel(lc, nck, u_ref, m_ref, q_ref, p_ref, a_ref, d_ref, v_ref, x_s, h_s):
    r = u_ref.shape[0] // lc
    bl = r // nck
    nhalf = x_s.shape[0] // 2
    step = lambda s: pl.ds(s, r, stride=lc)
    ucat = jnp.concatenate([u_ref[step(s), :] for s in range(lc)], axis=-1).astype(BF16)
    x = _dot(ucat, q_ref[...])
    for j in range(2 * nhalf):
        x_s[j] = x[:, j * LANES:(j + 1) * LANES]
    ar = [a_ref[0:1, j * LANES:(j + 1) * LANES] for j in range(nhalf)]
    ai = [a_ref[1:2, j * LANES:(j + 1) * LANES] for j in range(nhalf)]

    def body(ci, carry):
        rows = pl.ds(ci, bl, stride=nck)
        out = []
        for j in range(nhalf):
            sr, si = carry[j], carry[nhalf + j]
            h_s[j, rows, :] = sr
            h_s[nhalf + j, rows, :] = si
            out.append((ar[j] * sr - ai[j] * si + x_s[j, rows, :],
                        ar[j] * si + ai[j] * sr + x_s[nhalf + j, rows, :]))
        return tuple(o[0] for o in out) + tuple(o[1] for o in out)

    lax.fori_loop(0, nck, body, tuple(jnp.zeros((bl, LANES), F32) for _ in range(2 * nhalf)))
    h = jnp.concatenate([h_s[j] for j in range(2 * nhalf)], axis=-1).astype(BF16)
    y = _dot(ucat, m_ref[...]) + _dot(h, p_ref[...])
    for s in range(lc):
        v_ref[step(s), :] = jax.nn.gelu(y[:, s * LANES:(s + 1) * LANES] + d_ref[...] * u_ref[step(s), :])


def _s5_scan(u, mats, d_skip, layer, bsz, seq):
    n, w = u.shape
    lc = SSM_CHUNK
    nck = seq // lc
    mcat, qcat, pcat, a_pow = mats
    nb = mcat.shape[0]
    bl = _tile(bsz, 4)
    rows = bl * seq
    r = rows // lc
    per_block = lambda a: pl.BlockSpec((None,) + a.shape[1:], lambda i, j: (i,) + (0,) * (a.ndim - 1))
    io = pl.BlockSpec((rows, LANES), lambda i, j: (j, i))
    return pl.pallas_call(
        functools.partial(_s5_scan_kernel, lc, nck),
        out_shape=jax.ShapeDtypeStruct((n, w), F32),
        grid=(nb, bsz // bl),
        in_specs=[io, per_block(mcat), per_block(qcat), per_block(pcat), per_block(a_pow),
                  pl.BlockSpec((None, 1, LANES), lambda i, j: (layer, 0, i))],
        out_specs=io,
        scratch_shapes=[pltpu.VMEM((qcat.shape[2] // LANES, r, LANES), F32)] * 2,
        compiler_params=_cparams("parallel", "parallel"),
        name="s5_scan",
    )(u, mcat, qcat, pcat, a_pow, d_skip.reshape(-1, 1, w))


def _s5_glu_kernel(vrow_ref, v_ref, w_ref, b_ref, o_ref, wbf_ref):
    @pl.when(pl.program_id(1) == 0)
    def _():
        wbf_ref[...] = w_ref[...].astype(BF16)

    z = _dot(vrow_ref[...].astype(BF16), wbf_ref[...]) + b_ref[...]
    o_ref[...] = (v_ref[...] * jax.nn.sigmoid(z)).astype(o_ref.dtype)


def _s5_glu(v, w_glu, b_glu, layer, tm=1024, tn=512):
    n, w = v.shape
    tm = _tile(n, tm)
    tn = _tile(w, tn, LANES)
    return pl.pallas_call(
        _s5_glu_kernel,
        out_shape=jax.ShapeDtypeStruct((n, w), BF16),
        grid=(w // tn, n // tm),
        in_specs=[pl.BlockSpec((tm, w), lambda j, i: (i, 0)),
                  pl.BlockSpec((tm, tn), lambda j, i: (i, j)),
                  pl.BlockSpec((None, w, tn), lambda j, i: (layer, 0, j)),
                  pl.BlockSpec((None, 1, tn), lambda j, i: (layer, 0, j))],
        out_specs=pl.BlockSpec((tm, tn), lambda j, i: (i, j)),
        scratch_shapes=[pltpu.VMEM((w, tn), BF16)],
        compiler_params=_cparams("arbitrary", "arbitrary"),
        name="s5_glu",
    )(v, v, w_glu, b_glu.reshape(-1, 1, w))


def _s5_mixer(x_bf, bsz, seq, w_in, b_re, b_im, c_re, c_im, a_re, a_im, log_dt, d_skip,
              w_glu, b_glu, w_out, layer):
    n, d = x_bf.shape
    w = w_in.shape[2]
    u = _matmul(x_bf, w_in, layer, w, name="s5_in")
    mats = _s5_matrices(b_re[layer], b_im[layer], c_re[layer], c_im[layer], a_re[layer], a_im[layer], log_dt[layer])
    v = _s5_scan(u, mats, d_skip, layer, bsz, seq)
    hg = _s5_glu(v, w_glu, b_glu, layer)
    return _matmul(hg, w_out, layer, d, name="s5_out")


INT_MIN = -2 ** 31


def _fold_rows(x, op):
    return functools.reduce(op, [x[r:r + 8, :] for r in range(0, x.shape[0], 8)])


def _dsa_kernel(nsel, q_ref, qi_ref, wi_ref, k_ref, v_ref, ki_ref, o_ref,
                kbf, vext_t, kibf, qis, qs, w_t, slope_s, keys, selm, s_s, acc_s):
    qb = pl.program_id(1)
    blk = Q_BLOCK
    hd = ATT_HEAD_DIM
    nah = q_ref.shape[1] // hd
    nkb = qb + 1

    @pl.when(qb == 0)
    def _():
        kbf[...] = k_ref[...].astype(BF16)
        kibf[...] = ki_ref[...].astype(BF16)
        for j in range(vext_t.shape[0]):
            vext_t[j, :hd, :] = v_ref[j * blk:(j + 1) * blk, :].T.astype(BF16)
            vext_t[j, hd:, :] = jnp.ones((vext_t.shape[1] - hd, blk), BF16)
        for h in range(nah):
            slope_s[:, h * blk:(h + 1) * blk] = jnp.full((8, blk), 2.0 ** (-8.0 * (h + 1) / nah), F32)

    qi = qi_ref[...]
    for h in range(IDX_HEADS):
        qis[h * blk:(h + 1) * blk, :] = qi[:, h * IDX_DIM:(h + 1) * IDX_DIM].astype(BF16)
    q = q_ref[...]
    for h in range(nah):
        qs[h * blk:(h + 1) * blk, :] = q[:, h * hd:(h + 1) * hd].astype(BF16)
    w_t[...] = (wi_ref[...] * (IDX_HEADS ** -0.5 * IDX_DIM ** -0.5)).T

    row = lax.broadcasted_iota(I32, (blk, blk), 0)
    col = lax.broadcasted_iota(I32, (blk, blk), 1)
    tpos = qb * blk + col

    def score_body(kb, carry):
        off = pl.multiple_of(kb * blk, blk)
        dots = _dot_nt(kibf[pl.ds(off, blk), :], qis[...])
        sc = jnp.zeros((blk, blk), F32)
        for h in range(IDX_HEADS):
            sc = sc + jnp.maximum(dots[:, h * blk:(h + 1) * blk], 0.0) * w_t[h:h + 1, :]
        sc = jnp.where(sc == 0.0, 0.0, sc)
        bits = pltpu.bitcast(sc, I32)
        key = jnp.where(bits < 0, bits ^ 0x7FFFFFFF, bits)
        keys[kb] = jnp.where(off + row <= tpos, key, INT_MIN)
        return carry

    lax.fori_loop(0, nkb, score_body, 0)

    def count_ge(cands):
        def body(kb, accs):
            key = keys[kb]
            return tuple(acc + jnp.where(key >= cand, 1.0, 0.0) for acc, cand in zip(accs, cands))
        accs = lax.fori_loop(0, nkb, body, tuple(jnp.zeros((blk, blk), F32) for _ in cands))
        return [jnp.sum(_fold_rows(acc, jnp.add), axis=0, keepdims=True) for acc in accs]

    def bit_body(t, th):
        cand = th + jnp.left_shift(jnp.int32(1), 31 - t)
        return jnp.where(count_ge([cand])[0] >= nsel, cand, th)

    th = lax.fori_loop(0, 32, bit_body, jnp.full((1, blk), INT_MIN, I32))
    n_gt, n_ge = count_ge([th + 1, th])
    overflow = jnp.where((n_ge > nsel) & (th > INT_MIN), 1.0, 0.0)
    has_ties = jnp.max(overflow) > 0.0

    @pl.when(jnp.logical_not(has_ties))
    def _():
        floor = jnp.maximum(th, INT_MIN + 1)

        def body(kb, carry):
            selm[kb] = jnp.where(keys[kb] >= floor, 1.0, 0.0)
            return carry
        lax.fori_loop(0, nkb, body, 0)

    @pl.when(has_ties)
    def _():
        need = nsel - n_gt
        tri = jnp.where(col <= row, 1.0, 0.0).astype(BF16)

        def body(kb, run):
            key = keys[kb]
            eq = jnp.where(key == th, 1.0, 0.0)
            rank = run + _dot(tri, eq.astype(BF16))
            take = (key > th) | ((key == th) & (rank <= need))
            selm[kb] = jnp.where(take & (key > INT_MIN), 1.0, 0.0)
            return run + jnp.sum(eq, axis=0, keepdims=True)
        lax.fori_loop(0, nkb, body, jnp.zeros((1, blk), F32))

    kcol = lax.broadcasted_iota(I32, (blk, 1), 0)

    def max_body(kb, m8):
        off = pl.multiple_of(kb * blk, blk)
        kpos = (off + kcol).astype(F32)
        keep = jnp.tile(selm[kb], (1, nah)) > 0.5
        s = _dot_nt(kbf[pl.ds(off, blk), :], qs[...]) * (hd ** -0.5) + kpos * slope_s[0:1, :]
        s = jnp.where(keep, s, NEG_INF)
        s_s[kb] = s
        return jnp.maximum(m8, _fold_rows(s, jnp.maximum))

    m8 = lax.fori_loop(0, nkb, max_body, jnp.full((8, nah * blk), NEG_INF, F32))
    m = jnp.max(m8, axis=0, keepdims=True)
    acc_s[...] = jnp.zeros(acc_s.shape, F32)

    def acc_body(kb, carry):
        p = jnp.exp(s_s[kb] - m).astype(BF16)
        acc_s[...] += _dot(vext_t[kb], p)
        return carry

    lax.fori_loop(0, nkb, acc_body, 0)
    out_t = acc_s[:hd, :] / acc_s[hd:hd + 1, :]
    for h in range(nah):
        o_ref[:, h * hd:(h + 1) * hd] = out_t[:, h * blk:(h + 1) * blk].T.astype(o_ref.dtype)


def _dsa_attention(proj, bsz, seq, d):
    blk = Q_BLOCK
    nq = seq // blk
    nah = d // ATT_HEAD_DIM
    iw = IDX_HEADS * IDX_DIM
    assert iw % d == 0
    kcol = (iw + d) // LANES
    nsel = float(min(TOPK_MAX, seq // 4))
    rowblk = lambda width, cidx: pl.BlockSpec((blk, width), lambda b, i: (b * nq + i, cidx))
    seqblk = lambda cidx: pl.BlockSpec((seq, LANES), lambda b, i: (b, cidx))
    return pl.pallas_call(
        functools.partial(_dsa_kernel, nsel),
        out_shape=jax.ShapeDtypeStruct((bsz * seq, d), BF16),
        grid=(bsz, nq),
        in_specs=[rowblk(d, iw // d), rowblk(iw, 0), rowblk(LANES, kcol + 3),
                  seqblk(kcol), seqblk(kcol + 1), seqblk(kcol + 2)],
        out_specs=pl.BlockSpec((blk, d), lambda b, i: (b * nq + i, 0)),
        scratch_shapes=[pltpu.VMEM((seq, ATT_HEAD_DIM), BF16),
                        pltpu.VMEM((nq, 2 * ATT_HEAD_DIM, blk), BF16),
                        pltpu.VMEM((seq, IDX_DIM), BF16),
                        pltpu.VMEM((IDX_HEADS * blk, IDX_DIM), BF16),
                        pltpu.VMEM((nah * blk, ATT_HEAD_DIM), BF16),
                        pltpu.VMEM((LANES, blk), F32),
                        pltpu.VMEM((8, nah * blk), F32),
                        pltpu.VMEM((nq, blk, blk), I32),
                        pltpu.VMEM((nq, blk, blk), F32),
                        pltpu.VMEM((nq, blk, nah * blk), F32),
                        pltpu.VMEM((2 * ATT_HEAD_DIM, nah * blk), F32)],
        compiler_params=_cparams("parallel", "arbitrary"),
        name="dsa_attention",
    )(proj, proj, proj, proj, proj, proj)


def _dsa_mixer(x_bf, bsz, seq, w_in, w_out, layer):
    n, d = x_bf.shape
    hd = ATT_HEAD_DIM
    iw = IDX_HEADS * IDX_DIM
    w = w_in[layer]
    o_k, o_v, o_qi, o_ki, o_wi = d, d + hd, d + 2 * hd, d + 2 * hd + iw, d + 2 * hd + iw + IDX_DIM
    cols = [w[:, o_qi:o_ki], w[:, :o_k], w[:, o_k:o_v], w[:, o_v:o_qi], w[:, o_ki:o_wi], w[:, o_wi:]]
    width = sum(c.shape[1] for c in cols)
    pad = -(-width // (4 * LANES)) * (4 * LANES) - width
    w_re = jnp.concatenate(cols + [jnp.zeros((d, pad), F32)], axis=1)[None]
    proj = _matmul(x_bf, w_re, 0, width + pad, name="dsa_in")
    o = _dsa_attention(proj, bsz, seq, d)
    return _matmul(o, w_out, layer, d, name="dsa_out")


def kernel(x, ln_g, ln_b, moe_rg_w, moe_rg_b, moe_re_w, moe_re_b, moe_w_gate, moe_w_up, moe_w_down,
           gdn_w_in, gdn_conv_w, gdn_a_log, gdn_dt_bias, gdn_norm_g, gdn_w_out,
           ssm_w_in, ssm_b_re, ssm_b_im, ssm_c_re, ssm_c_im, ssm_a_re, ssm_a_im, ssm_log_dt,
           ssm_d, ssm_w_glu, ssm_b_glu, ssm_w_out, dsa_w_in, dsa_w_out):
    bsz, seq, d = x.shape
    depth = ln_g.shape[0]
    alpha = (2.0 * depth) ** 0.25
    xf = x.reshape(bsz * seq, d)
    xb = xf.astype(BF16)
    counts = [0, 0, 0]
    for layer in range(depth):
        kind = layer % 3
        i = counts[kind]
        counts[kind] += 1
        if kind == 0:
            h = _gdn_mixer(xb, bsz, seq, gdn_w_in, gdn_conv_w, gdn_a_log, gdn_dt_bias, gdn_norm_g, gdn_w_out, i)
        elif kind == 1:
            h = _s5_mixer(xb, bsz, seq, ssm_w_in, ssm_b_re, ssm_b_im, ssm_c_re, ssm_c_im, ssm_a_re, ssm_a_im,
                          ssm_log_dt, ssm_d, ssm_w_glu, ssm_b_glu, ssm_w_out, i)
        else:
            h = _dsa_mixer(xb, bsz, seq, dsa_w_in, dsa_w_out, i)
        x1 = _ln_mixer(xf, h, ln_g, ln_b, layer, 0, alpha)
        xf, xb = _moe_layer(x1, ln_g, ln_b, layer, alpha, moe_rg_w, moe_rg_b, moe_re_w, moe_re_b,
                            moe_w_gate, moe_w_up, moe_w_down)
    return xf.reshape(bsz, seq, d)
```

```python
import functools
import math

import jax
import jax.numpy as jnp
from jax import lax
from jax.experimental import pallas as pl
from jax.experimental.pallas import tpu as pltpu

F32 = jnp.float32
BF16 = jnp.bfloat16
I32 = jnp.int32

LANES = 128
VMEM_LIMIT = 56 * 1024 * 1024

LN_EPS = 1e-5
RMS_EPS = 1e-6
NEG_INF = -1e30

GDN_HEAD_DIM = 128
GDN_CONV = 4
GDN_CHUNK = 64
SSM_GROUP = 16
SSM_STATE = 64
SSM_CHUNK = 8
ATT_HEAD_DIM = 128
IDX_HEADS = 16
IDX_DIM = 128
TOPK_MAX = 256
Q_BLOCK = 128
MOE_GROUPS = 4
MOE_EPG = 8
MOE_BLOCK = 256


def _cparams(*sem):
    return pltpu.CompilerParams(dimension_semantics=sem, vmem_limit_bytes=VMEM_LIMIT)


def _tile(n, pref, mult=1):
    t = min(n, pref) // mult * mult
    while t > mult and n % t:
        t -= mult
    assert t > 0 and n % t == 0, (n, pref, mult)
    return t


def _dot(a, b):
    return jnp.dot(a, b, preferred_element_type=F32)


def _dot_nt(a, b):
    return lax.dot_general(a, b, (((1,), (1,)), ((), ())), preferred_element_type=F32)


def _dot_tn(a, b):
    return lax.dot_general(a, b, (((0,), (0,)), ((), ())), preferred_element_type=F32)


def _split_bf16(x):
    hi = x.astype(BF16)
    lo = (x - hi.astype(F32)).astype(BF16)
    return hi, lo


def _dot3(a, b):
    ah, al = _split_bf16(a)
    bh, bl = _split_bf16(b)
    return _dot(ah, bh) + (_dot(ah, bl) + _dot(al, bh))


def _mm_kernel(x_ref, w_ref, o_ref, wbf_ref):
    @pl.when(pl.program_id(1) == 0)
    def _():
        wbf_ref[...] = w_ref[...].astype(BF16)

    o_ref[...] = _dot(x_ref[...], wbf_ref[...]).astype(o_ref.dtype)


def _matmul(x, w, layer, n_cols, *, col0=0, out_dtype=F32, tm=1024, tn=512, name="matmul"):
    m, k = x.shape
    tm = _tile(m, tm)
    tn = _tile(n_cols, tn, LANES)
    assert col0 % tn == 0
    jb = col0 // tn
    return pl.pallas_call(
        _mm_kernel,
        out_shape=jax.ShapeDtypeStruct((m, n_cols), out_dtype),
        grid=(n_cols // tn, m // tm),
        in_specs=[pl.BlockSpec((tm, k), lambda j, i: (i, 0)),
                  pl.BlockSpec((None, k, tn), lambda j, i: (layer, 0, j + jb))],
        out_specs=pl.BlockSpec((tm, tn), lambda j, i: (i, j)),
        scratch_shapes=[pltpu.VMEM((k, tn), BF16)],
        compiler_params=_cparams("arbitrary", "arbitrary"),
        name=name,
    )(x, w)


def _layer_norm_rows(v, g, b):
    mu = jnp.mean(v, axis=-1, keepdims=True)
    c = v - mu
    var = jnp.mean(c * c, axis=-1, keepdims=True)
    return c * lax.rsqrt(var + LN_EPS) * g + b


def _ln_mixer_kernel(alpha, x_ref, h_ref, g_ref, b_ref, rw_ref, rb_ref, o_ref, route_ref, cnt_ref):
    y = _layer_norm_rows(alpha * x_ref[...] + h_ref[...], g_ref[...], b_ref[...])
    o_ref[...] = y
    route, hit = _route_rows(y, rw_ref[...], rb_ref[...])
    route_ref[...] = route

    @pl.when(pl.program_id(0) == 0)
    def _():
        cnt_ref[...] = jnp.zeros(cnt_ref.shape, F32)

    cnt_ref[...] += jnp.sum(jnp.where(hit, 1.0, 0.0), axis=0, keepdims=True)


def _ln_moe_kernel(alpha, x_ref, y_ref, route_ref, g_ref, b_ref, o_ref, obf_ref):
    h = y_ref[0] * route_ref[:, 2:3] + y_ref[1] * route_ref[:, 3:4]
    y = _layer_norm_rows(alpha * x_ref[...] + h, g_ref[...], b_ref[...])
    o_ref[...] = y
    obf_ref[...] = y.astype(BF16)


def _ln_mixer(x, h, ln_g, ln_b, layer, which, alpha, rw, rb, tm=512):
    n, d = x.shape
    tm = _tile(n, tm)
    row = pl.BlockSpec((tm, d), lambda i: (i, 0))
    par = pl.BlockSpec((None, 1, d), lambda i: (layer * 2 + which, 0, 0))
    lanes = pl.BlockSpec((tm, LANES), lambda i: (i, 0))
    fixed = lambda shape: pl.BlockSpec(shape, lambda i: (0, 0))
    return pl.pallas_call(
        functools.partial(_ln_mixer_kernel, alpha),
        out_shape=(jax.ShapeDtypeStruct((n, d), F32), jax.ShapeDtypeStruct((n, LANES), F32),
                   jax.ShapeDtypeStruct((1, LANES), F32)),
        grid=(n // tm,),
        in_specs=[row, row, par, par, fixed((d, LANES)), fixed((1, LANES))],
        out_specs=(row, lanes, fixed((1, LANES))),
        compiler_params=_cparams("arbitrary"),
        name="ln_mixer",
    )(x, h, ln_g.reshape(-1, 1, d), ln_b.reshape(-1, 1, d), rw, rb)


def _ln_moe(x, y2, route, ln_g, ln_b, layer, which, alpha, tm=512):
    n, d = x.shape
    tm = _tile(n, tm)
    row = pl.BlockSpec((tm, d), lambda i: (i, 0))
    par = pl.BlockSpec((None, 1, d), lambda i: (layer * 2 + which, 0, 0))
    return pl.pallas_call(
        functools.partial(_ln_moe_kernel, alpha),
        out_shape=(jax.ShapeDtypeStruct((n, d), F32), jax.ShapeDtypeStruct((n, d), BF16)),
        grid=(n // tm,),
        in_specs=[row, pl.BlockSpec((2, tm, d), lambda i: (0, i, 0)),
                  pl.BlockSpec((tm, LANES), lambda i: (i, 0)), par, par],
        out_specs=(row, row),
        compiler_params=_cparams("parallel"),
        name="ln_moe",
    )(x, y2, route, ln_g.reshape(-1, 1, d), ln_b.reshape(-1, 1, d))


def _first_lane_where(cond, lane):
    return jnp.min(jnp.where(cond, lane, LANES), axis=-1, keepdims=True)


def _route_rows(x, w, b):
    logits = _dot3(x, w) + b
    lane = lax.broadcasted_iota(I32, logits.shape, 1)
    gmask = lane < MOE_GROUPS
    lg = jnp.where(gmask, logits, NEG_INF)
    eg = jnp.where(gmask, jnp.exp(lg - jnp.max(lg, axis=-1, keepdims=True)), 0.0)
    pg = eg / jnp.sum(eg, axis=-1, keepdims=True)
    p_g = jnp.max(pg, axis=-1, keepdims=True)
    g_idx = _first_lane_where(gmask & (pg == p_g), lane)
    lo = MOE_GROUPS + g_idx * MOE_EPG
    emask = (lane >= lo) & (lane < lo + MOE_EPG)
    le = jnp.where(emask, logits, NEG_INF)
    ee = jnp.where(emask, jnp.exp(le - jnp.max(le, axis=-1, keepdims=True)), 0.0)
    pe = jnp.where(emask, ee / jnp.sum(ee, axis=-1, keepdims=True), -1.0)
    p1 = jnp.max(pe, axis=-1, keepdims=True)
    i1 = _first_lane_where(pe == p1, lane)
    pe2 = jnp.where(lane == i1, -1.0, pe)
    p2 = jnp.max(pe2, axis=-1, keepdims=True)
    i2 = _first_lane_where(pe2 == p2, lane)
    den = p1 + p2
    e1 = (i1 - MOE_GROUPS).astype(F32)
    e2 = (i2 - MOE_GROUPS).astype(F32)
    out = jnp.where(lane == 0, e1, jnp.where(lane == 1, e2,
          jnp.where(lane == 2, p_g * p1 / den, jnp.where(lane == 3, p_g * p2 / den, 0.0))))
    hit = (lane == i1 - MOE_GROUPS) | (lane == i2 - MOE_GROUPS)
    return out, hit


def _plan_kernel(n_tok, route_ref, pstart_ref, tok_ref, dst_ref, run_s, dvm_s, dsm_s, sem):
    i = pl.program_id(0)
    tm = route_ref.shape[0]
    cap = tok_ref.shape[0]
    unroll = 8

    @pl.when(i == 0)
    def _():
        run_s[...] = jnp.zeros(run_s.shape, F32)

        def clear(g, c):
            for j in range(unroll):
                tok_ref[g * unroll + j] = 0
                dst_ref[g * unroll + j] = 0
            return c
        lax.fori_loop(0, cap // unroll, clear, 0)

    route = route_ref[...]
    lane = lax.broadcasted_iota(I32, route.shape, 1).astype(F32)
    oh0 = jnp.where(lane == route[:, 0:1], 1.0, 0.0)
    oh1 = jnp.where(lane == route[:, 1:2], 1.0, 0.0)
    both = oh0 + oh1
    r_i = lax.broadcasted_iota(I32, (tm, tm), 0)
    c_i = lax.broadcasted_iota(I32, (tm, tm), 1)
    before = _dot(jnp.where(c_i < r_i, 1.0, 0.0).astype(BF16), both.astype(BF16))
    base = pstart_ref[...] + run_s[...] + before
    d0 = jnp.sum(oh0 * base, axis=-1, keepdims=True)
    d1 = jnp.sum(oh1 * base, axis=-1, keepdims=True)
    run_s[...] += jnp.sum(both, axis=0, keepdims=True)
    lane_i = lax.broadcasted_iota(I32, route.shape, 1)
    dcols = jnp.where(lane_i == 0, d0, jnp.where(lane_i == 1, d1, 0.0))
    dvm_s[...] = dcols.T[0:8, :].astype(I32)
    cp = pltpu.make_async_copy(dvm_s, dsm_s, sem)
    cp.start()
    cp.wait()

    def place(g, c):
        for j in range(unroll):
            r = g * unroll + j
            tok = i * tm + r
            s0 = dsm_s[0, r]
            s1 = dsm_s[1, r]
            tok_ref[s0] = tok
            dst_ref[s0] = tok
            tok_ref[s1] = tok
            dst_ref[s1] = n_tok + tok
        return c
    lax.fori_loop(0, tm // unroll, place, 0)


def _moe_plan(route, pad_start, cap, tm=256):
    n = route.shape[0]
    tm = _tile(n, tm, LANES)
    smem_out = pl.BlockSpec(memory_space=pltpu.SMEM)
    return pl.pallas_call(
        functools.partial(_plan_kernel, n),
        out_shape=(jax.ShapeDtypeStruct((cap,), I32), jax.ShapeDtypeStruct((cap,), I32)),
        grid=(n // tm,),
        in_specs=[pl.BlockSpec((tm, LANES), lambda i: (i, 0)), pl.BlockSpec((1, LANES), lambda i: (0, 0))],
        out_specs=(smem_out, smem_out),
        scratch_shapes=[pltpu.VMEM((1, LANES), F32), pltpu.VMEM((8, tm), I32), pltpu.SMEM((8, tm), I32),
                        pltpu.SemaphoreType.DMA],
        compiler_params=_cparams("arbitrary"),
        name="moe_plan",
    )(route, pad_start)


def _expert_kernel(be_ref, nv_ref, tok_ref, dst_ref, nact_ref,
                   x_hbm, wg_ref, wu_ref, wd_ref, y_hbm,
                   xbuf, ybuf, wgb, wub, wdb, gsem, ssem):
    i = pl.program_id(0)
    nb = pl.num_programs(0)
    nact = nact_ref[0]
    blk = xbuf.shape[1]
    slot = i % 2
    group = 8

    def gather_copy(tok, sl, r, n=1):
        return pltpu.make_async_copy(x_hbm.at[pl.ds(tok, n), :], xbuf.at[sl, pl.ds(r, n), :], gsem.at[sl])

    def scatter_copy(dst, sl, r, n=1):
        return pltpu.make_async_copy(ybuf.at[sl, pl.ds(r, n), :], y_hbm.at[pl.ds(dst, n), :], ssem.at[sl])

    def for_rows(b, fn):
        n = nv_ref[b]
        full = n // group

        def grp(g, c):
            for j in range(group):
                fn(g * group + j)
            return c
        lax.fori_loop(0, full, grp, 0)

        def one(r, c):
            fn(r)
            return c
        lax.fori_loop(full * group, n, one, 0)

    def wait_rows(b, whole, row):
        @pl.when(nv_ref[b] == blk)
        def _():
            whole.wait()

        @pl.when(nv_ref[b] < blk)
        def _():
            for_rows(b, lambda r: row(r).wait())

    def start_gather(b, sl):
        for_rows(b, lambda r: gather_copy(tok_ref[b * blk + r], sl, r).start())

    def wait_gather(b, sl):
        wait_rows(b, gather_copy(0, sl, 0, blk), lambda r: gather_copy(0, sl, r))

    def start_scatter(b, sl):
        for_rows(b, lambda r: scatter_copy(dst_ref[b * blk + r], sl, r).start())

    def wait_scatter(b, sl):
        wait_rows(b, scatter_copy(0, sl, 0, blk), lambda r: scatter_copy(0, sl, r))

    @pl.when(i == 0)
    def _():
        xbuf[...] = jnp.zeros(xbuf.shape, xbuf.dtype)

        @pl.when(nact > 0)
        def _():
            start_gather(0, 0)

    @pl.when(i < nact)
    def _():
        wait_gather(i, slot)

        @pl.when(i + 1 < nact)
        def _():
            start_gather(i + 1, 1 - slot)

        prev = be_ref[jnp.maximum(i - 1, 0)]

        @pl.when((i == 0) | (be_ref[i] != prev))
        def _():
            wgb[...] = wg_ref[...].astype(BF16)
            wub[...] = wu_ref[...].astype(BF16)
            wdb[...] = wd_ref[...].astype(BF16)

        @pl.when(i >= 2)
        def _():
            wait_scatter(i - 2, slot)

        xb = xbuf[slot].astype(BF16)
        a = _dot(xb, wgb[...])
        u = _dot(xb, wub[...])
        h = (jax.nn.silu(a) * u).astype(BF16)
        ybuf[slot] = _dot(h, wdb[...])
        start_scatter(i, slot)

    @pl.when(i == nb - 1)
    def _():
        for back in (2, 1):
            b = nact - back

            @pl.when(b >= 0)
            def _():
                wait_scatter(b, b % 2)


def _moe_dispatch(route, counts_row, n_exp, blk):
    n = route.shape[0]
    n_asg = 2 * n
    experts = jnp.arange(n_exp, dtype=I32)
    counts = counts_row[0, :n_exp].astype(I32)
    padded = (counts + blk - 1) // blk * blk
    pad_end = jnp.cumsum(padded)
    pad_start = pad_end - padded
    n_blocks = (n_asg + n_exp * (blk - 1) + blk - 1) // blk
    block_start = jnp.arange(n_blocks, dtype=I32) * blk
    block_expert = jnp.minimum(jnp.sum(pad_end[None, :] <= block_start[:, None], axis=1, dtype=I32), n_exp - 1)
    mine = block_expert[:, None] == experts[None, :]
    pick = lambda v: jnp.sum(jnp.where(mine, v[None, :], 0), axis=1, dtype=I32)
    block_nvalid = jnp.clip(pick(counts) - (block_start - pick(pad_start)), 0, blk).astype(I32)
    n_active = (pad_end[-1] // blk).astype(I32).reshape(1)
    pstart_row = jnp.zeros((1, LANES), F32).at[0, :n_exp].set(pad_start.astype(F32))
    slot_tok, slot_dst = _moe_plan(route, pstart_row, n_blocks * blk)
    return block_expert, block_nvalid, slot_tok, slot_dst, n_active


def _moe_experts(x, route, counts_row, w_gate, w_up, w_down, layer):
    n, d = x.shape
    n_exp, ff = w_gate.shape[1], w_gate.shape[3]
    blk = MOE_BLOCK
    be, nv, slot_tok, slot_dst, nact = _moe_dispatch(route, counts_row, n_exp, blk)
    n_blocks = be.shape[0]
    grid_spec = pltpu.PrefetchScalarGridSpec(
        num_scalar_prefetch=5,
        grid=(n_blocks,),
        in_specs=[pl.BlockSpec(memory_space=pl.ANY),
                  pl.BlockSpec((None, None, d, ff), lambda i, be, *_: (layer, be[i], 0, 0)),
                  pl.BlockSpec((None, None, d, ff), lambda i, be, *_: (layer, be[i], 0, 0)),
                  pl.BlockSpec((None, None, ff, d), lambda i, be, *_: (layer, be[i], 0, 0))],
        out_specs=pl.BlockSpec(memory_space=pl.ANY),
        scratch_shapes=[pltpu.VMEM((2, blk, d), F32),
                        pltpu.VMEM((2, blk, d), F32),
                        pltpu.VMEM((d, ff), BF16), pltpu.VMEM((d, ff), BF16), pltpu.VMEM((ff, d), BF16),
                        pltpu.SemaphoreType.DMA((2,)), pltpu.SemaphoreType.DMA((2,))],
    )
    y = pl.pallas_call(
        _expert_kernel,
        out_shape=jax.ShapeDtypeStruct((2 * n, d), F32),
        grid_spec=grid_spec,
        compiler_params=_cparams("arbitrary"),
        name="moe_experts",
    )(be, nv, slot_tok, slot_dst, nact, x, w_gate, w_up, w_down)
    return y.reshape(2, n, d)


def _mixer_norm_moe(x, h, ln_g, ln_b, layer, alpha, rg_w, rg_b, re_w, re_b, w_gate, w_up, w_down):
    d = x.shape[1]
    n_r = MOE_GROUPS + MOE_GROUPS * MOE_EPG
    rw = jnp.zeros((d, LANES), F32).at[:, :MOE_GROUPS].set(rg_w[layer]).at[:, MOE_GROUPS:n_r].set(re_w[layer])
    rb = jnp.zeros((1, LANES), F32).at[0, :MOE_GROUPS].set(rg_b[layer]).at[0, MOE_GROUPS:n_r].set(re_b[layer])
    x1, route, counts_row = _ln_mixer(x, h, ln_g, ln_b, layer, 0, alpha, rw, rb)
    y2 = _moe_experts(x1, route, counts_row, w_gate, w_up, w_down, layer)
    return _ln_moe(x1, y2, route, ln_g, ln_b, layer, 1, alpha)


def _gdn_gate_kernel(nh, ab_ref, alog_ref, dtb_ref, gcb_ref, gct_ref):
    ab = ab_ref[...]
    tt = ab.shape[0]
    lane = lax.broadcasted_iota(I32, ab.shape, 1)
    pos = lax.broadcasted_iota(I32, ab.shape, 0) % GDN_CHUNK
    g = -jnp.exp(alog_ref[...]) * jax.nn.softplus(ab + dtb_ref[...])
    s = 1
    while s < GDN_CHUNK:
        g = g + jnp.where(pos >= s, pltpu.roll(g, s, 0), 0.0)
        s *= 2
    out = jnp.where(lane < nh, g, jnp.where(lane < 2 * nh, jax.nn.sigmoid(ab), 0.0))
    gcb_ref[...] = out
    for j in range(tt // LANES):
        gct_ref[j] = out[j * LANES:(j + 1) * LANES, :].T


def _gdn_gates(ab, alog_p, dtb_p, nh, tt=512):
    n = ab.shape[0]
    tt = _tile(n, tt, LANES)
    return pl.pallas_call(
        functools.partial(_gdn_gate_kernel, nh),
        out_shape=(jax.ShapeDtypeStruct((n, LANES), F32),
                   jax.ShapeDtypeStruct((n // LANES, LANES, LANES), F32)),
        grid=(n // tt,),
        in_specs=[pl.BlockSpec((tt, LANES), lambda i: (i, 0)),
                  pl.BlockSpec((1, LANES), lambda i: (0, 0)),
                  pl.BlockSpec((1, LANES), lambda i: (0, 0))],
        out_specs=(pl.BlockSpec((tt, LANES), lambda i: (i, 0)),
                   pl.BlockSpec((tt // LANES, LANES, LANES), lambda i: (i, 0, 0))),
        compiler_params=_cparams("parallel"),
        name="gdn_gates",
    )(ab, alog_p, dtb_p)


def _gdn_conv_kernel(hpb, ncb, x_ref, w_ref, o_ref, carry_ref):
    t = pl.program_id(1)
    j = pl.program_id(2)
    x = x_ref[...]
    tt, tc = x.shape

    @pl.when(t == 0)
    def _():
        carry_ref[j] = jnp.zeros((8, tc), F32)

    prev = jnp.tile(carry_ref[j], (tt // 8, 1))
    carry_ref[j] = x[tt - 8:tt, :]
    row = lax.broadcasted_iota(I32, x.shape, 0)
    acc = x * w_ref[GDN_CONV - 1:GDN_CONV, :]
    for s in range(1, GDN_CONV):
        xs = jnp.where(row < s, pltpu.roll(prev, s, 0), pltpu.roll(x, s, 0))
        acc = acc + xs * w_ref[GDN_CONV - 1 - s:GDN_CONV - s, :]
    y = jax.nn.silu(acc)
    kind = j // (ncb // 3)
    post = jnp.where(kind == 0, GDN_HEAD_DIM ** -0.5, 1.0)
    for hh in range(hpb):
        yh = y[:, hh * LANES:(hh + 1) * LANES]
        inv = lax.rsqrt(jnp.sum(yh * yh, axis=-1, keepdims=True) + RMS_EPS)
        o_ref[hh] = yh * (jnp.where(kind == 2, 1.0, inv) * post)


def _gdn_conv(proj, conv_w, layer, bsz, seq, nh, tt=256):
    hpb = _tile(nh, 8)
    tc = hpb * LANES
    ncb = 3 * nh // hpb
    hb = nh // hpb
    tt = _tile(seq, tt, 8)
    nt = seq // tt
    return pl.pallas_call(
        functools.partial(_gdn_conv_kernel, hpb, ncb),
        out_shape=jax.ShapeDtypeStruct((3, bsz, nh, seq, LANES), F32),
        grid=(bsz, nt, ncb),
        in_specs=[pl.BlockSpec((tt, tc), lambda b, t, j: (b * nt + t, j)),
                  pl.BlockSpec((None, GDN_CONV, tc), lambda b, t, j: (layer, 0, j))],
        out_specs=pl.BlockSpec((None, None, hpb, tt, LANES), lambda b, t, j: (j // hb, b, j % hb, t, 0)),
        scratch_shapes=[pltpu.VMEM((ncb, 8, tc), F32)],
        compiler_params=_cparams("arbitrary", "arbitrary", "arbitrary"),
        name="gdn_conv",
    )(proj, conv_w)


def _head_column(tile, lane, idx):
    return jnp.sum(jnp.where(lane == idx, tile, 0.0), axis=-1, keepdims=True)


def _gdn_local_kernel(nh, q_ref, k_ref, gcb_ref, gct_ref, l_ref, a_ref):
    h = pl.program_id(1)
    c = GDN_CHUNK
    lane = lax.broadcasted_iota(I32, (LANES, LANES), 1)
    row = lax.broadcasted_iota(I32, (LANES, LANES), 0)
    lower = ((row // c) == (lane // c)) & (lane <= row)
    for s in range(q_ref.shape[0] // LANES):
        rows = slice(s * LANES, (s + 1) * LANES)
        q = q_ref[rows, :]
        k = k_ref[rows, :]
        gcb = gcb_ref[rows, :]
        gcol = _head_column(gcb, lane, h)
        bcol = _head_column(gcb, lane, nh + h)
        grow = gct_ref[s, pl.ds(h, 1), :]
        decay = jnp.where(lower, jnp.exp(jnp.where(lower, gcol - grow, 0.0)), 0.0)
        kbf = k.astype(BF16)
        kk = _dot_nt((k * bcol).astype(BF16), kbf)
        qk = _dot_nt(q.astype(BF16), kbf)
        l_ref[s] = jnp.where(lane < row, kk * decay, 0.0)
        a_ref[s] = (qk * decay).astype(BF16)


def _gdn_local(qkv, gcb, gct, bsz, seq, nh, tt=2048):
    c = LANES
    tt = _tile(seq, tt, LANES)
    nt = seq // tt
    nc = seq // c
    qk_spec = lambda which: pl.BlockSpec((None, None, None, tt, LANES), lambda b, h, t: (which, b, h, t, 0))
    out_spec = pl.BlockSpec((None, None, tt // c, c, c), lambda b, h, t: (b, h, t, 0, 0))
    return pl.pallas_call(
        functools.partial(_gdn_local_kernel, nh),
        out_shape=(jax.ShapeDtypeStruct((bsz, nh, nc, c, c), F32),
                   jax.ShapeDtypeStruct((bsz, nh, nc, c, c), BF16)),
        grid=(bsz, nh, nt),
        in_specs=[qk_spec(0), qk_spec(1),
                  pl.BlockSpec((tt, LANES), lambda b, h, t: (b * nt + t, 0)),
                  pl.BlockSpec((tt // LANES, LANES, LANES), lambda b, h, t: (b * nt + t, 0, 0))],
        out_specs=(out_spec, out_spec),
        compiler_params=_cparams("parallel", "parallel", "parallel"),
        name="gdn_local",
    )(qkv, qkv, gcb, gct)


def _gdn_solve_kernel(l_ref, t_ref, lt_s, tt_s):
    c = GDN_CHUNK
    pr = 2 * c
    npb = l_ref.shape[0] // pr
    for i in range(c):
        both = l_ref[pl.ds(i, npb, stride=pr), :] + l_ref[pl.ds(c + i, npb, stride=pr), :]
        bt = both.T
        lt_s[0, i] = bt[:c, :]
        lt_s[1, i] = bt[c:, :]

    halves = (0, 1)
    for g in halves:
        tt_s[g, 0] = jnp.where(lax.broadcasted_iota(I32, (c, npb), 0) == 0, 1.0, 0.0)
    for ib in range(c // 8):
        ext = 8 * (ib + 1)
        sub = lax.broadcasted_iota(I32, (ext, npb), 0)

        def row_body(ii, carry, ib=ib, ext=ext, sub=sub):
            i = ib * 8 + ii
            accs = [jnp.where(sub == i, 1.0, 0.0) for _ in halves]
            for mb in range(ib):
                em = 8 * (mb + 1)
                for g in halves:
                    t = sum(lt_s[g, i, m:m + 1, :] * tt_s[g, m, 0:em, :] for m in range(8 * mb, em))
                    accs[g] = jnp.concatenate([accs[g][:em] - t, accs[g][em:]], axis=0)

            def m_body(mm, a):
                m = ib * 8 + mm
                return tuple(a[g] - lt_s[g, i, pl.ds(m, 1), :] * tt_s[g, m, 0:ext, :] for g in halves)
            accs = lax.fori_loop(0, ii, m_body, tuple(accs))
            for g in halves:
                tt_s[g, i, 0:ext, :] = accs[g]
                if ext < c:
                    tt_s[g, i, ext:c, :] = jnp.zeros((c - ext, npb), F32)
            return carry

        lax.fori_loop(1 if ib == 0 else 0, 8, row_body, 0)

    lane = lax.broadcasted_iota(I32, (npb, pr), 1)
    for i in range(c):
        rows = jnp.concatenate([tt_s[0, i], tt_s[1, i]], axis=0).T
        t_ref[pl.ds(i, npb, stride=pr), :] = jnp.where(lane < c, rows, 0.0)
        t_ref[pl.ds(c + i, npb, stride=pr), :] = jnp.where(lane >= c, rows, 0.0)


def _gdn_solve(lpair, pairs_per_step=128):
    c = GDN_CHUNK
    pr = 2 * c
    n_pairs = lpair.shape[0] // pr
    npb = _tile(n_pairs, pairs_per_step)
    spec = pl.BlockSpec((npb * pr, pr), lambda i: (i, 0))
    return pl.pallas_call(
        _gdn_solve_kernel,
        out_shape=jax.ShapeDtypeStruct(lpair.shape, F32),
        grid=(n_pairs // npb,),
        in_specs=[spec],
        out_specs=spec,
        scratch_shapes=[pltpu.VMEM((2, c, c, npb), F32), pltpu.VMEM((2, c, c, npb), F32)],
        compiler_params=_cparams("parallel"),
        name="gdn_solve",
    )(lpair)


def _gdn_recur_kernel(nh, q_ref, k_ref, v_ref, z_ref, gcb_ref, tm_ref, at_ref, ng_ref, o_ref, st_ref):
    hblk = pl.program_id(1)
    c = GDN_CHUNK
    pr = 2 * c
    hb, tt, dv = v_ref.shape
    lane = lax.broadcasted_iota(I32, (c, LANES), 1)

    @pl.when(pl.program_id(2) == 0)
    def _():
        st_ref[...] = jnp.zeros(st_ref.shape, F32)

    def body(pi, carry):
        for half in range(2):
            rows = pl.ds(pl.multiple_of(pi * pr, pr) + half * c, c)
            blk = slice(half * c, (half + 1) * c)
            gcb = gcb_ref[rows, :]
            for j in range(hb):
                h = hblk * hb + j
                q = q_ref[j, rows, :]
                k = k_ref[j, rows, :]
                v = v_ref[j, rows, :]
                gcol = _head_column(gcb, lane, h)
                bcol = _head_column(gcb, lane, nh + h)
                glast = gcol[c - 1:c, :]
                eg = jnp.exp(gcol)
                kb = k * bcol
                sol = _dot3(tm_ref[j, pi, blk, blk], jnp.concatenate([v * bcol, kb * eg], axis=-1))
                u = sol[:, :dv]
                w = sol[:, dv:]
                state = st_ref[j]
                sb = state.astype(BF16)
                v_new = u - _dot(w.astype(BF16), sb)
                vnb = v_new.astype(BF16)
                o = _dot((q * eg).astype(BF16), sb) + _dot(at_ref[j, pi, blk, blk], vnb)
                kd = k * jnp.exp(glast - gcol)
                st_ref[j] = state * jnp.exp(glast) + _dot_tn(kd.astype(BF16), vnb)
                o = o * lax.rsqrt(jnp.mean(o * o, axis=-1, keepdims=True) + RMS_EPS) * ng_ref[...]
                o = o * jax.nn.silu(z_ref[rows, j * dv:(j + 1) * dv])
                o_ref[rows, j * dv:(j + 1) * dv] = o.astype(o_ref.dtype)
        return carry

    lax.fori_loop(0, tt // pr, body, 0)


def _gdn_recur(qkv, proj, gcb, tm, at, norm_g, layer, bsz, seq, nh, hb=16, tt=256):
    c = 2 * GDN_CHUNK
    hb = _tile(nh, hb)
    tt = _tile(seq, tt, c)
    nt = seq // tt
    head = lambda which: pl.BlockSpec((None, None, hb, tt, LANES), lambda b, h, t: (which, b, h, t, 0))
    mat = pl.BlockSpec((None, hb, tt // c, c, c), lambda b, h, t: (b, h, t, 0, 0))
    return pl.pallas_call(
        functools.partial(_gdn_recur_kernel, nh),
        out_shape=jax.ShapeDtypeStruct((bsz * seq, nh * LANES), BF16),
        grid=(bsz, nh // hb, nt),
        in_specs=[head(0), head(1), head(2),
                  pl.BlockSpec((tt, hb * LANES), lambda b, h, t: (b * nt + t, 3 * nh // hb + h)),
                  pl.BlockSpec((tt, LANES), lambda b, h, t: (b * nt + t, 0)),
                  mat, mat,
                  pl.BlockSpec((None, 1, LANES), lambda b, h, t: (layer, 0, 0))],
        out_specs=pl.BlockSpec((tt, hb * LANES), lambda b, h, t: (b * nt + t, h)),
        scratch_shapes=[pltpu.VMEM((hb, GDN_HEAD_DIM, LANES), F32)],
        compiler_params=_cparams("parallel", "parallel", "arbitrary"),
        name="gdn_recur",
    )(qkv, qkv, qkv, proj, gcb, tm, at, norm_g.reshape(-1, 1, LANES))


def _pad_lanes(v):
    return jnp.zeros((1, LANES), F32).at[0, :v.shape[0]].set(v)


def _gdn_mixer(x_bf, bsz, seq, w_in, conv_w, a_log, dt_bias, norm_g, w_out, layer):
    n, d = x_bf.shape
    nh = d // GDN_HEAD_DIM
    c = GDN_CHUNK
    proj = _matmul(x_bf, w_in, layer, 4 * d, tn=1024, name="gdn_in")
    w_ab = jnp.zeros((1, d, LANES), F32).at[0, :, :2 * nh].set(w_in[layer, :, 4 * d:])
    ab = _matmul(x_bf, w_ab, 0, LANES, name="gdn_in_ab")
    gcb, gct = _gdn_gates(ab, _pad_lanes(a_log[layer]), _pad_lanes(dt_bias[layer]), nh)
    qkv = _gdn_conv(proj, conv_w, layer, bsz, seq, nh)
    lpair, at = _gdn_local(qkv, gcb, gct, bsz, seq, nh)
    tm = _gdn_solve(lpair.reshape(-1, 2 * c)).reshape(lpair.shape)
    og = _gdn_recur(qkv, proj, gcb, tm, at, norm_g, layer, bsz, seq, nh)
    return _matmul(og, w_out, layer, d, name="gdn_out")


def _s5_matrices(b_re, b_im, c_re, c_im, a_re, a_im, log_dt):
    lc = SSM_CHUNK
    g, p, nch = b_re.shape
    dt = jnp.exp(log_dt)[:, None]
    mag = jnp.exp(a_re * dt)
    ang = a_im * dt
    lb_re = mag * jnp.cos(ang)
    lb_im = mag * jnp.sin(ang)
    den = jnp.square(a_re) + jnp.square(a_im)
    f_re = ((lb_re - 1.0) * a_re + lb_im * a_im) / den
    f_im = (lb_im * a_re - (lb_re - 1.0) * a_im) / den
    bb_re = f_re[..., None] * b_re - f_im[..., None] * b_im
    bb_im = f_re[..., None] * b_im + f_im[..., None] * b_re
    dd = jnp.arange(lc + 1, dtype=F32)[:, None, None]
    pw_mag = jnp.exp(dd * (a_re * dt))
    pw_re = pw_mag * jnp.cos(dd * ang)
    pw_im = pw_mag * jnp.sin(dd * ang)
    gpb = LANES // nch
    nb = g // gpb
    wide = gpb * p
    rows = lambda a: a.reshape(nb, LANES, p)
    lanes = lambda a: jnp.transpose(a.reshape(lc + 1, nb, wide), (1, 0, 2))
    bt_re = rows(jnp.transpose(bb_re, (0, 2, 1)))
    bt_im = rows(jnp.transpose(bb_im, (0, 2, 1)))
    pw_re, pw_im = lanes(pw_re), lanes(pw_im)
    width = lc * LANES
    per_block = lambda a: pl.BlockSpec((None,) + a.shape[1:], lambda i: (i, 0, 0))
    ins = (rows(c_re), rows(c_im), bt_re, bt_im, pw_re, pw_im)
    out = jax.ShapeDtypeStruct((nb, width, width), BF16)
    assert 2 * wide == width
    mcat, qcat, pcat = pl.pallas_call(
        functools.partial(_s5_operator_kernel, lc),
        out_shape=(out, out, out),
        grid=(nb,),
        in_specs=[per_block(a) for a in ins],
        out_specs=(per_block(out), per_block(out), per_block(out)),
        compiler_params=_cparams("parallel"),
        name="s5_operators",
    )(*ins)
    a_pow = jnp.stack([pw_re[:, lc], pw_im[:, lc]], axis=1)
    return mcat, qcat, pcat, a_pow


def _dot3_nt(a, b):
    ah, al = _split_bf16(a)
    bh, bl = _split_bf16(b)
    return _dot_nt(ah, bh) + (_dot_nt(ah, bl) + _dot_nt(al, bh))


def _s5_operator_kernel(lc, c_re_ref, c_im_ref, b_re_ref, b_im_ref, pw_re_ref, pw_im_ref, m_ref, q_ref, p_ref):
    p = c_re_ref.shape[1]
    wide = pw_re_ref.shape[1]
    row = lax.broadcasted_iota(I32, (LANES, wide), 0)
    lane = lax.broadcasted_iota(I32, (LANES, wide), 1)
    own = (row // SSM_GROUP) == (lane // p)

    def embed(ref):
        return jnp.where(own, jnp.concatenate([ref[...]] * (wide // p), axis=1), 0.0)

    c_re, c_im, b_re, b_im = embed(c_re_ref), embed(c_im_ref), embed(b_re_ref), embed(b_im_ref)
    taps = []
    for d in range(lc + 1):
        pr = pw_re_ref[d:d + 1, :]
        pi = pw_im_ref[d:d + 1, :]
        cp_re = c_re * pr - c_im * pi
        cp_im = c_re * pi + c_im * pr
        if d < lc:
            taps.append(_dot3_nt(b_re, cp_re) - _dot3_nt(b_im, cp_im))
            s = lc - 1 - d
            q_ref[s * LANES:(s + 1) * LANES, :wide] = (b_re * pr - b_im * pi).astype(BF16)
            q_ref[s * LANES:(s + 1) * LANES, wide:] = (b_re * pi + b_im * pr).astype(BF16)
        if d > 0:
            t = d - 1
            p_ref[:wide, t * LANES:(t + 1) * LANES] = cp_re.T.astype(BF16)
            p_ref[wide:, t * LANES:(t + 1) * LANES] = (-cp_im).T.astype(BF16)
    zero = jnp.zeros((LANES, LANES), BF16)
    for s in range(lc):
        for t in range(lc):
            m_ref[s * LANES:(s + 1) * LANES, t * LANES:(t + 1) * LANES] = (
                taps[t - s].astype(BF16) if t >= s else zero)


def _s5_scan_kernel(lc, nck, u_ref, m_ref, q_ref, p_ref, a_ref, d_ref, v_ref, x_s, h_s):
    r = u_ref.shape[0] // lc
    bl = r // nck
    nhalf = x_s.shape[0] // 2
    step = lambda s: pl.ds(s, r, stride=lc)
    ucat = jnp.concatenate([u_ref[step(s), :] for s in range(lc)], axis=-1).astype(BF16)
    x = _dot(ucat, q_ref[...])
    for j in range(2 * nhalf):
        x_s[j] = x[:, j * LANES:(j + 1) * LANES]
    ar = [a_ref[0:1, j * LANES:(j + 1) * LANES] for j in range(nhalf)]
    ai = [a_ref[1:2, j * LANES:(j + 1) * LANES] for j in range(nhalf)]

    def body(ci, carry):
        rows = pl.ds(ci, bl, stride=nck)
        out = []
        for j in range(nhalf):
            sr, si = carry[j], carry[nhalf + j]
            h_s[j, rows, :] = sr
            h_s[nhalf + j, rows, :] = si
            out.append((ar[j] * sr - ai[j] * si + x_s[j, rows, :],
                        ar[j] * si + ai[j] * sr + x_s[nhalf + j, rows, :]))
        return tuple(o[0] for o in out) + tuple(o[1] for o in out)

    lax.fori_loop(0, nck, body, tuple(jnp.zeros((bl, LANES), F32) for _ in range(2 * nhalf)))
    h = jnp.concatenate([h_s[j] for j in range(2 * nhalf)], axis=-1).astype(BF16)
    y = _dot(ucat, m_ref[...]) + _dot(h, p_ref[...])
    for s in range(lc):
        v_ref[step(s), :] = jax.nn.gelu(y[:, s * LANES:(s + 1) * LANES] + d_ref[...] * u_ref[step(s), :])


def _s5_scan(u, mats, d_skip, layer, bsz, seq):
    n, w = u.shape
    lc = SSM_CHUNK
    nck = seq // lc
    mcat, qcat, pcat, a_pow = mats
    nb = mcat.shape[0]
    bl = _tile(bsz, 4)
    rows = bl * seq
    r = rows // lc
    per_block = lambda a: pl.BlockSpec((None,) + a.shape[1:], lambda i, j: (i,) + (0,) * (a.ndim - 1))
    io = pl.BlockSpec((rows, LANES), lambda i, j: (j, i))
    return pl.pallas_call(
        functools.partial(_s5_scan_kernel, lc, nck),
        out_shape=jax.ShapeDtypeStruct((n, w), F32),
        grid=(nb, bsz // bl),
        in_specs=[io, per_block(mcat), per_block(qcat), per_block(pcat), per_block(a_pow),
                  pl.BlockSpec((None, 1, LANES), lambda i, j: (layer, 0, i))],
        out_specs=io,
        scratch_shapes=[pltpu.VMEM((qcat.shape[2] // LANES, r, LANES), F32)] * 2,
        compiler_params=_cparams("parallel", "parallel"),
        name="s5_scan",
    )(u, mcat, qcat, pcat, a_pow, d_skip.reshape(-1, 1, w))


def _s5_glu_kernel(vrow_ref, v_ref, w_ref, b_ref, o_ref, wbf_ref):
    @pl.when(pl.program_id(1) == 0)
    def _():
        wbf_ref[...] = w_ref[...].astype(BF16)

    z = _dot(vrow_ref[...].astype(BF16), wbf_ref[...]) + b_ref[...]
    o_ref[...] = (v_ref[...] * jax.nn.sigmoid(z)).astype(o_ref.dtype)


def _s5_glu(v, w_glu, b_glu, layer, tm=1024, tn=512):
    n, w = v.shape
    tm = _tile(n, tm)
    tn = _tile(w, tn, LANES)
    return pl.pallas_call(
        _s5_glu_kernel,
        out_shape=jax.ShapeDtypeStruct((n, w), BF16),
        grid=(w // tn, n // tm),
        in_specs=[pl.BlockSpec((tm, w), lambda j, i: (i, 0)),
                  pl.BlockSpec((tm, tn), lambda j, i: (i, j)),
                  pl.BlockSpec((None, w, tn), lambda j, i: (layer, 0, j)),
                  pl.BlockSpec((None, 1, tn), lambda j, i: (layer, 0, j))],
        out_specs=pl.BlockSpec((tm, tn), lambda j, i: (i, j)),
        scratch_shapes=[pltpu.VMEM((w, tn), BF16)],
        compiler_params=_cparams("arbitrary", "arbitrary"),
        name="s5_glu",
    )(v, v, w_glu, b_glu.reshape(-1, 1, w))


def _s5_mixer(x_bf, bsz, seq, w_in, b_re, b_im, c_re, c_im, a_re, a_im, log_dt, d_skip,
              w_glu, b_glu, w_out, layer):
    n, d = x_bf.shape
    w = w_in.shape[2]
    u = _matmul(x_bf, w_in, layer, w, name="s5_in")
    mats = _s5_matrices(b_re[layer], b_im[layer], c_re[layer], c_im[layer], a_re[layer], a_im[layer], log_dt[layer])
    v = _s5_scan(u, mats, d_skip, layer, bsz, seq)
    hg = _s5_glu(v, w_glu, b_glu, layer)
    return _matmul(hg, w_out, layer, d, name="s5_out")


INT_MIN = -2 ** 31


def _fold_rows(x, op):
    return functools.reduce(op, [x[r:r + 8, :] for r in range(0, x.shape[0], 8)])


def _dsa_kernel(nsel, q_ref, qi_ref, wi_ref, k_ref, v_ref, ki_ref, o_ref,
                kbf, vext_t, kibf, qis, qs, w_t, slope_s, keys, selm, s_s, acc_s):
    qb = pl.program_id(1)
    blk = Q_BLOCK
    hd = ATT_HEAD_DIM
    nah = q_ref.shape[1] // hd
    nkb = qb + 1

    @pl.when(qb == 0)
    def _():
        kbf[...] = k_ref[...].astype(BF16)
        kibf[...] = ki_ref[...].astype(BF16)
        for j in range(vext_t.shape[0]):
            vext_t[j, :hd, :] = v_ref[j * blk:(j + 1) * blk, :].T.astype(BF16)
            vext_t[j, hd:, :] = jnp.ones((vext_t.shape[1] - hd, blk), BF16)
        for h in range(nah):
            slope_s[:, h * blk:(h + 1) * blk] = jnp.full((8, blk), 2.0 ** (-8.0 * (h + 1) / nah), F32)

    qi = qi_ref[...]
    for h in range(IDX_HEADS):
        qis[h * blk:(h + 1) * blk, :] = qi[:, h * IDX_DIM:(h + 1) * IDX_DIM].astype(BF16)
    q = q_ref[...]
    for h in range(nah):
        qs[h * blk:(h + 1) * blk, :] = q[:, h * hd:(h + 1) * hd].astype(BF16)
    w_t[...] = (wi_ref[...] * (IDX_HEADS ** -0.5 * IDX_DIM ** -0.5)).T

    row = lax.broadcasted_iota(I32, (blk, blk), 0)
    col = lax.broadcasted_iota(I32, (blk, blk), 1)
    tpos = qb * blk + col

    def score_body(kb, carry):
        off = pl.multiple_of(kb * blk, blk)
        dots = _dot_nt(kibf[pl.ds(off, blk), :], qis[...])
        sc = jnp.zeros((blk, blk), F32)
        for h in range(IDX_HEADS):
            sc = sc + jnp.maximum(dots[:, h * blk:(h + 1) * blk], 0.0) * w_t[h:h + 1, :]
        sc = jnp.where(sc == 0.0, 0.0, sc)
        bits = pltpu.bitcast(sc, I32)
        key = jnp.where(bits < 0, bits ^ 0x7FFFFFFF, bits)
        keys[kb] = jnp.where(off + row <= tpos, key, INT_MIN)
        return carry

    lax.fori_loop(0, nkb, score_body, 0)

    def count_ge(cands):
        def body(kb, accs):
            key = keys[kb]
            return tuple(acc + jnp.where(key >= cand, 1.0, 0.0) for acc, cand in zip(accs, cands))
        accs = lax.fori_loop(0, nkb, body, tuple(jnp.zeros((blk, blk), F32) for _ in cands))
        return [jnp.sum(_fold_rows(acc, jnp.add), axis=0, keepdims=True) for acc in accs]

    def bit_body(t, th):
        cand = th + jnp.left_shift(jnp.int32(1), 31 - t)
        return jnp.where(count_ge([cand])[0] >= nsel, cand, th)

    th = lax.fori_loop(0, 32, bit_body, jnp.full((1, blk), INT_MIN, I32))
    n_gt, n_ge = count_ge([th + 1, th])
    overflow = jnp.where((n_ge > nsel) & (th > INT_MIN), 1.0, 0.0)
    has_ties = jnp.max(overflow) > 0.0

    @pl.when(jnp.logical_not(has_ties))
    def _():
        floor = jnp.maximum(th, INT_MIN + 1)

        def body(kb, carry):
            selm[kb] = jnp.where(keys[kb] >= floor, 1.0, 0.0)
            return carry
        lax.fori_loop(0, nkb, body, 0)

    @pl.when(has_ties)
    def _():
        need = nsel - n_gt
        tri = jnp.where(col <= row, 1.0, 0.0).astype(BF16)

        def body(kb, run):
            key = keys[kb]
            eq = jnp.where(key == th, 1.0, 0.0)
            rank = run + _dot(tri, eq.astype(BF16))
            take = (key > th) | ((key == th) & (rank <= need))
            selm[kb] = jnp.where(take & (key > INT_MIN), 1.0, 0.0)
            return run + jnp.sum(eq, axis=0, keepdims=True)
        lax.fori_loop(0, nkb, body, jnp.zeros((1, blk), F32))

    kcol = lax.broadcasted_iota(I32, (blk, 1), 0)

    def max_body(kb, m8):
        off = pl.multiple_of(kb * blk, blk)
        kpos = (off + kcol).astype(F32)
        keep = jnp.tile(selm[kb], (1, nah)) > 0.5
        s = _dot_nt(kbf[pl.ds(off, blk), :], qs[...]) * (hd ** -0.5) + kpos * slope_s[0:1, :]
        s = jnp.where(keep, s, NEG_INF)
        s_s[kb] = s
        return jnp.maximum(m8, _fold_rows(s, jnp.maximum))

    m8 = lax.fori_loop(0, nkb, max_body, jnp.full((8, nah * blk), NEG_INF, F32))
    m = jnp.max(m8, axis=0, keepdims=True)
    acc_s[...] = jnp.zeros(acc_s.shape, F32)

    def acc_body(kb, carry):
        p = jnp.exp(s_s[kb] - m).astype(BF16)
        acc_s[...] += _dot(vext_t[kb], p)
        return carry

    lax.fori_loop(0, nkb, acc_body, 0)
    out_t = acc_s[:hd, :] / acc_s[hd:hd + 1, :]
    for h in range(nah):
        o_ref[:, h * hd:(h + 1) * hd] = out_t[:, h * blk:(h + 1) * blk].T.astype(o_ref.dtype)


def _dsa_attention(proj, bsz, seq, d):
    blk = Q_BLOCK
    nq = seq // blk
    nah = d // ATT_HEAD_DIM
    iw = IDX_HEADS * IDX_DIM
    assert iw % d == 0
    kcol = (iw + d) // LANES
    nsel = float(min(TOPK_MAX, seq // 4))
    rowblk = lambda width, cidx: pl.BlockSpec((blk, width), lambda b, i: (b * nq + i, cidx))
    seqblk = lambda cidx: pl.BlockSpec((seq, LANES), lambda b, i: (b, cidx))
    return pl.pallas_call(
        functools.partial(_dsa_kernel, nsel),
        out_shape=jax.ShapeDtypeStruct((bsz * seq, d), BF16),
        grid=(bsz, nq),
        in_specs=[rowblk(d, iw // d), rowblk(iw, 0), rowblk(LANES, kcol + 3),
                  seqblk(kcol), seqblk(kcol + 1), seqblk(kcol + 2)],
        out_specs=pl.BlockSpec((blk, d), lambda b, i: (b * nq + i, 0)),
        scratch_shapes=[pltpu.VMEM((seq, ATT_HEAD_DIM), BF16),
                        pltpu.VMEM((nq, 2 * ATT_HEAD_DIM, blk), BF16),
                        pltpu.VMEM((seq, IDX_DIM), BF16),
                        pltpu.VMEM((IDX_HEADS * blk, IDX_DIM), BF16),
                        pltpu.VMEM((nah * blk, ATT_HEAD_DIM), BF16),
                        pltpu.VMEM((LANES, blk), F32),
                        pltpu.VMEM((8, nah * blk), F32),
                        pltpu.VMEM((nq, blk, blk), I32),
                        pltpu.VMEM((nq, blk, blk), F32),
                        pltpu.VMEM((nq, blk, nah * blk), F32),
                        pltpu.VMEM((2 * ATT_HEAD_DIM, nah * blk), F32)],
        compiler_params=_cparams("parallel", "arbitrary"),
        name="dsa_attention",
    )(proj, proj, proj, proj, proj, proj)


def _dsa_mixer(x_bf, bsz, seq, w_in, w_out, layer):
    n, d = x_bf.shape
    hd = ATT_HEAD_DIM
    iw = IDX_HEADS * IDX_DIM
    w = w_in[layer]
    o_k, o_v, o_qi, o_ki, o_wi = d, d + hd, d + 2 * hd, d + 2 * hd + iw, d + 2 * hd + iw + IDX_DIM
    cols = [w[:, o_qi:o_ki], w[:, :o_k], w[:, o_k:o_v], w[:, o_v:o_qi], w[:, o_ki:o_wi], w[:, o_wi:]]
    width = sum(c.shape[1] for c in cols)
    pad = -(-width // (4 * LANES)) * (4 * LANES) - width
    w_re = jnp.concatenate(cols + [jnp.zeros((d, pad), F32)], axis=1)[None]
    proj = _matmul(x_bf, w_re, 0, width + pad, name="dsa_in")
    o = _dsa_attention(proj, bsz, seq, d)
    return _matmul(o, w_out, layer, d, name="dsa_out")


def kernel(x, ln_g, ln_b, moe_rg_w, moe_rg_b, moe_re_w, moe_re_b, moe_w_gate, moe_w_up, moe_w_down,
           gdn_w_in, gdn_conv_w, gdn_a_log, gdn_dt_bias, gdn_norm_g, gdn_w_out,
           ssm_w_in, ssm_b_re, ssm_b_im, ssm_c_re, ssm_c_im, ssm_a_re, ssm_a_im, ssm_log_dt,
           ssm_d, ssm_w_glu, ssm_b_glu, ssm_w_out, dsa_w_in, dsa_w_out):
    bsz, seq, d = x.shape
    depth = ln_g.shape[0]
    alpha = (2.0 * depth) ** 0.25
    xf = x.reshape(bsz * seq, d)
    xb = xf.astype(BF16)
    counts = [0, 0, 0]
    for layer in range(depth):
        kind = layer % 3
        i = counts[kind]
        counts[kind] += 1
        if kind == 0:
            h = _gdn_mixer(xb, bsz, seq, gdn_w_in, gdn_conv_w, gdn_a_log, gdn_dt_bias, gdn_norm_g, gdn_w_out, i)
        elif kind == 1:
            h = _s5_mixer(xb, bsz, seq, ssm_w_in, ssm_b_re, ssm_b_im, ssm_c_re, ssm_c_im, ssm_a_re, ssm_a_im,
                          ssm_log_dt, ssm_d, ssm_w_glu, ssm_b_glu, ssm_w_out, i)
        else:
            h = _dsa_mixer(xb, bsz, seq, dsa_w_in, dsa_w_out, i)
        xf, xb = _mixer_norm_moe(xf, h, ln_g, ln_b, layer, alpha, moe_rg_w, moe_rg_b, moe_re_w, moe_re_b,
                                 moe_w_gate, moe_w_up, moe_w_down)
    return xf.reshape(bsz, seq, d)
```

```python
import functools
import math

import jax
import jax.numpy as jnp
from jax import lax
from jax.experimental import pallas as pl
from jax.experimental.pallas import tpu as pltpu

F32 = jnp.float32
BF16 = jnp.bfloat16
I32 = jnp.int32

LANES = 128
VMEM_LIMIT = 56 * 1024 * 1024

LN_EPS = 1e-5
RMS_EPS = 1e-6
NEG_INF = -1e30

GDN_HEAD_DIM = 128
GDN_CONV = 4
GDN_CHUNK = 64
SSM_GROUP = 16
SSM_STATE = 64
SSM_CHUNK = 8
ATT_HEAD_DIM = 128
IDX_HEADS = 16
IDX_DIM = 128
TOPK_MAX = 256
Q_BLOCK = 128
MOE_GROUPS = 4
MOE_EPG = 8
MOE_BLOCK = 256


def _cparams(*sem):
    return pltpu.CompilerParams(dimension_semantics=sem, vmem_limit_bytes=VMEM_LIMIT)


def _tile(n, pref, mult=1):
    t = min(n, pref) // mult * mult
    while t > mult and n % t:
        t -= mult
    assert t > 0 and n % t == 0, (n, pref, mult)
    return t


def _dot(a, b):
    return jnp.dot(a, b, preferred_element_type=F32)


def _dot_nt(a, b):
    return lax.dot_general(a, b, (((1,), (1,)), ((), ())), preferred_element_type=F32)


def _dot_tn(a, b):
    return lax.dot_general(a, b, (((0,), (0,)), ((), ())), preferred_element_type=F32)


def _split_bf16(x):
    hi = x.astype(BF16)
    lo = (x - hi.astype(F32)).astype(BF16)
    return hi, lo


def _dot3(a, b):
    ah, al = _split_bf16(a)
    bh, bl = _split_bf16(b)
    return _dot(ah, bh) + (_dot(ah, bl) + _dot(al, bh))


def _mm_kernel(x_ref, w_ref, o_ref, wbf_ref):
    @pl.when(pl.program_id(1) == 0)
    def _():
        wbf_ref[...] = w_ref[...].astype(BF16)

    o_ref[...] = _dot(x_ref[...], wbf_ref[...]).astype(o_ref.dtype)


def _matmul(x, w, layer, n_cols, *, col0=0, out_dtype=F32, tm=1024, tn=1024, name="matmul"):
    m, k = x.shape
    tm = _tile(m, tm)
    tn = _tile(n_cols, tn, LANES)
    assert col0 % tn == 0
    jb = col0 // tn
    return pl.pallas_call(
        _mm_kernel,
        out_shape=jax.ShapeDtypeStruct((m, n_cols), out_dtype),
        grid=(n_cols // tn, m // tm),
        in_specs=[pl.BlockSpec((tm, k), lambda j, i: (i, 0)),
                  pl.BlockSpec((None, k, tn), lambda j, i: (layer, 0, j + jb))],
        out_specs=pl.BlockSpec((tm, tn), lambda j, i: (i, j)),
        scratch_shapes=[pltpu.VMEM((k, tn), BF16)],
        compiler_params=_cparams("arbitrary", "arbitrary"),
        name=name,
    )(x, w)


def _layer_norm_rows(v, g, b):
    mu = jnp.mean(v, axis=-1, keepdims=True)
    c = v - mu
    var = jnp.mean(c * c, axis=-1, keepdims=True)
    return c * lax.rsqrt(var + LN_EPS) * g + b


def _ln_mixer_kernel(alpha, x_ref, h_ref, g_ref, b_ref, rw_ref, rb_ref, o_ref, route_ref, cnt_ref):
    y = _layer_norm_rows(alpha * x_ref[...] + h_ref[...], g_ref[...], b_ref[...])
    o_ref[...] = y
    route, hit = _route_rows(y, rw_ref[...], rb_ref[...])
    route_ref[...] = route

    @pl.when(pl.program_id(0) == 0)
    def _():
        cnt_ref[...] = jnp.zeros(cnt_ref.shape, F32)

    cnt_ref[...] += jnp.sum(jnp.where(hit, 1.0, 0.0), axis=0, keepdims=True)


def _ln_moe_kernel(alpha, x_ref, y_ref, route_ref, g_ref, b_ref, o_ref, obf_ref):
    h = y_ref[0] * route_ref[:, 2:3] + y_ref[1] * route_ref[:, 3:4]
    y = _layer_norm_rows(alpha * x_ref[...] + h, g_ref[...], b_ref[...])
    o_ref[...] = y
    obf_ref[...] = y.astype(BF16)


def _ln_mixer(x, h, ln_g, ln_b, layer, which, alpha, rw, rb, tm=512):
    n, d = x.shape
    tm = _tile(n, tm)
    row = pl.BlockSpec((tm, d), lambda i: (i, 0))
    par = pl.BlockSpec((None, 1, d), lambda i: (layer * 2 + which, 0, 0))
    lanes = pl.BlockSpec((tm, LANES), lambda i: (i, 0))
    fixed = lambda shape: pl.BlockSpec(shape, lambda i: (0, 0))
    return pl.pallas_call(
        functools.partial(_ln_mixer_kernel, alpha),
        out_shape=(jax.ShapeDtypeStruct((n, d), F32), jax.ShapeDtypeStruct((n, LANES), F32),
                   jax.ShapeDtypeStruct((1, LANES), F32)),
        grid=(n // tm,),
        in_specs=[row, row, par, par, fixed((d, LANES)), fixed((1, LANES))],
        out_specs=(row, lanes, fixed((1, LANES))),
        compiler_params=_cparams("arbitrary"),
        name="ln_mixer",
    )(x, h, ln_g.reshape(-1, 1, d), ln_b.reshape(-1, 1, d), rw, rb)


def _ln_moe(x, y2, route, ln_g, ln_b, layer, which, alpha, tm=512):
    n, d = x.shape
    tm = _tile(n, tm)
    row = pl.BlockSpec((tm, d), lambda i: (i, 0))
    par = pl.BlockSpec((None, 1, d), lambda i: (layer * 2 + which, 0, 0))
    return pl.pallas_call(
        functools.partial(_ln_moe_kernel, alpha),
        out_shape=(jax.ShapeDtypeStruct((n, d), F32), jax.ShapeDtypeStruct((n, d), BF16)),
        grid=(n // tm,),
        in_specs=[row, pl.BlockSpec((2, tm, d), lambda i: (0, i, 0)),
                  pl.BlockSpec((tm, LANES), lambda i: (i, 0)), par, par],
        out_specs=(row, row),
        compiler_params=_cparams("parallel"),
        name="ln_moe",
    )(x, y2, route, ln_g.reshape(-1, 1, d), ln_b.reshape(-1, 1, d))


def _first_lane_where(cond, lane):
    return jnp.min(jnp.where(cond, lane, LANES), axis=-1, keepdims=True)


def _route_rows(x, w, b):
    logits = _dot3(x, w) + b
    lane = lax.broadcasted_iota(I32, logits.shape, 1)
    gmask = lane < MOE_GROUPS
    lg = jnp.where(gmask, logits, NEG_INF)
    eg = jnp.where(gmask, jnp.exp(lg - jnp.max(lg, axis=-1, keepdims=True)), 0.0)
    pg = eg / jnp.sum(eg, axis=-1, keepdims=True)
    p_g = jnp.max(pg, axis=-1, keepdims=True)
    g_idx = _first_lane_where(gmask & (pg == p_g), lane)
    lo = MOE_GROUPS + g_idx * MOE_EPG
    emask = (lane >= lo) & (lane < lo + MOE_EPG)
    le = jnp.where(emask, logits, NEG_INF)
    ee = jnp.where(emask, jnp.exp(le - jnp.max(le, axis=-1, keepdims=True)), 0.0)
    pe = jnp.where(emask, ee / jnp.sum(ee, axis=-1, keepdims=True), -1.0)
    p1 = jnp.max(pe, axis=-1, keepdims=True)
    i1 = _first_lane_where(pe == p1, lane)
    pe2 = jnp.where(lane == i1, -1.0, pe)
    p2 = jnp.max(pe2, axis=-1, keepdims=True)
    i2 = _first_lane_where(pe2 == p2, lane)
    den = p1 + p2
    e1 = (i1 - MOE_GROUPS).astype(F32)
    e2 = (i2 - MOE_GROUPS).astype(F32)
    out = jnp.where(lane == 0, e1, jnp.where(lane == 1, e2,
          jnp.where(lane == 2, p_g * p1 / den, jnp.where(lane == 3, p_g * p2 / den, 0.0))))
    hit = (lane == i1 - MOE_GROUPS) | (lane == i2 - MOE_GROUPS)
    return out, hit


def _plan_kernel(n_tok, route_ref, pstart_ref, tok_ref, dst_ref, run_s, dvm_s, dsm_s, sem):
    i = pl.program_id(0)
    tm = route_ref.shape[0]
    cap = tok_ref.shape[0]
    unroll = 8

    @pl.when(i == 0)
    def _():
        run_s[...] = jnp.zeros(run_s.shape, F32)

        def clear(g, c):
            for j in range(unroll):
                tok_ref[g * unroll + j] = 0
                dst_ref[g * unroll + j] = 0
            return c
        lax.fori_loop(0, cap // unroll, clear, 0)

    route = route_ref[...]
    lane = lax.broadcasted_iota(I32, route.shape, 1).astype(F32)
    oh0 = jnp.where(lane == route[:, 0:1], 1.0, 0.0)
    oh1 = jnp.where(lane == route[:, 1:2], 1.0, 0.0)
    both = oh0 + oh1
    r_i = lax.broadcasted_iota(I32, (tm, tm), 0)
    c_i = lax.broadcasted_iota(I32, (tm, tm), 1)
    before = _dot(jnp.where(c_i < r_i, 1.0, 0.0).astype(BF16), both.astype(BF16))
    base = pstart_ref[...] + run_s[...] + before
    d0 = jnp.sum(oh0 * base, axis=-1, keepdims=True)
    d1 = jnp.sum(oh1 * base, axis=-1, keepdims=True)
    run_s[...] += jnp.sum(both, axis=0, keepdims=True)
    lane_i = lax.broadcasted_iota(I32, route.shape, 1)
    dcols = jnp.where(lane_i == 0, d0, jnp.where(lane_i == 1, d1, 0.0))
    dvm_s[...] = dcols.T[0:8, :].astype(I32)
    cp = pltpu.make_async_copy(dvm_s, dsm_s, sem)
    cp.start()
    cp.wait()

    def place(g, c):
        for j in range(unroll):
            r = g * unroll + j
            tok = i * tm + r
            s0 = dsm_s[0, r]
            s1 = dsm_s[1, r]
            tok_ref[s0] = tok
            dst_ref[s0] = tok
            tok_ref[s1] = tok
            dst_ref[s1] = n_tok + tok
        return c
    lax.fori_loop(0, tm // unroll, place, 0)


def _moe_plan(route, pad_start, cap, tm=256):
    n = route.shape[0]
    tm = _tile(n, tm, LANES)
    smem_out = pl.BlockSpec(memory_space=pltpu.SMEM)
    return pl.pallas_call(
        functools.partial(_plan_kernel, n),
        out_shape=(jax.ShapeDtypeStruct((cap,), I32), jax.ShapeDtypeStruct((cap,), I32)),
        grid=(n // tm,),
        in_specs=[pl.BlockSpec((tm, LANES), lambda i: (i, 0)), pl.BlockSpec((1, LANES), lambda i: (0, 0))],
        out_specs=(smem_out, smem_out),
        scratch_shapes=[pltpu.VMEM((1, LANES), F32), pltpu.VMEM((8, tm), I32), pltpu.SMEM((8, tm), I32),
                        pltpu.SemaphoreType.DMA],
        compiler_params=_cparams("arbitrary"),
        name="moe_plan",
    )(route, pad_start)


def _expert_kernel(be_ref, nv_ref, tok_ref, dst_ref, nact_ref,
                   x_hbm, wg_ref, wu_ref, wd_ref, y_hbm,
                   xbuf, ybuf, wgb, wub, wdb, gsem, ssem):
    i = pl.program_id(0)
    nb = pl.num_programs(0)
    nact = nact_ref[0]
    blk = xbuf.shape[1]
    slot = i % 2
    group = 8

    def gather_copy(tok, sl, r, n=1):
        return pltpu.make_async_copy(x_hbm.at[pl.ds(tok, n), :], xbuf.at[sl, pl.ds(r, n), :], gsem.at[sl])

    def scatter_copy(dst, sl, r, n=1):
        return pltpu.make_async_copy(ybuf.at[sl, pl.ds(r, n), :], y_hbm.at[pl.ds(dst, n), :], ssem.at[sl])

    def for_rows(b, fn):
        n = nv_ref[b]
        full = n // group

        def grp(g, c):
            for j in range(group):
                fn(g * group + j)
            return c
        lax.fori_loop(0, full, grp, 0)

        def one(r, c):
            fn(r)
            return c
        lax.fori_loop(full * group, n, one, 0)

    def wait_rows(b, whole, row):
        @pl.when(nv_ref[b] == blk)
        def _():
            whole.wait()

        @pl.when(nv_ref[b] < blk)
        def _():
            for_rows(b, lambda r: row(r).wait())

    def start_gather(b, sl):
        for_rows(b, lambda r: gather_copy(tok_ref[b * blk + r], sl, r).start())

    def wait_gather(b, sl):
        wait_rows(b, gather_copy(0, sl, 0, blk), lambda r: gather_copy(0, sl, r))

    def start_scatter(b, sl):
        for_rows(b, lambda r: scatter_copy(dst_ref[b * blk + r], sl, r).start())

    def wait_scatter(b, sl):
        wait_rows(b, scatter_copy(0, sl, 0, blk), lambda r: scatter_copy(0, sl, r))

    @pl.when(i == 0)
    def _():
        xbuf[...] = jnp.zeros(xbuf.shape, xbuf.dtype)

        @pl.when(nact > 0)
        def _():
            start_gather(0, 0)

    @pl.when(i < nact)
    def _():
        wait_gather(i, slot)

        @pl.when(i + 1 < nact)
        def _():
            start_gather(i + 1, 1 - slot)

        prev = be_ref[jnp.maximum(i - 1, 0)]

        @pl.when((i == 0) | (be_ref[i] != prev))
        def _():
            wgb[...] = wg_ref[...].astype(BF16)
            wub[...] = wu_ref[...].astype(BF16)
            wdb[...] = wd_ref[...].astype(BF16)

        @pl.when(i >= 2)
        def _():
            wait_scatter(i - 2, slot)

        xb = xbuf[slot].astype(BF16)
        a = _dot(xb, wgb[...])
        u = _dot(xb, wub[...])
        h = (jax.nn.silu(a) * u).astype(BF16)
        ybuf[slot] = _dot(h, wdb[...])
        start_scatter(i, slot)

    @pl.when(i == nb - 1)
    def _():
        for back in (2, 1):
            b = nact - back

            @pl.when(b >= 0)
            def _():
                wait_scatter(b, b % 2)


def _moe_dispatch(route, counts_row, n_exp, blk):
    n = route.shape[0]
    n_asg = 2 * n
    experts = jnp.arange(n_exp, dtype=I32)
    counts = counts_row[0, :n_exp].astype(I32)
    padded = (counts + blk - 1) // blk * blk
    pad_end = jnp.cumsum(padded)
    pad_start = pad_end - padded
    n_blocks = (n_asg + n_exp * (blk - 1) + blk - 1) // blk
    block_start = jnp.arange(n_blocks, dtype=I32) * blk
    block_expert = jnp.minimum(jnp.sum(pad_end[None, :] <= block_start[:, None], axis=1, dtype=I32), n_exp - 1)
    mine = block_expert[:, None] == experts[None, :]
    pick = lambda v: jnp.sum(jnp.where(mine, v[None, :], 0), axis=1, dtype=I32)
    block_nvalid = jnp.clip(pick(counts) - (block_start - pick(pad_start)), 0, blk).astype(I32)
    n_active = (pad_end[-1] // blk).astype(I32).reshape(1)
    pstart_row = jnp.zeros((1, LANES), F32).at[0, :n_exp].set(pad_start.astype(F32))
    slot_tok, slot_dst = _moe_plan(route, pstart_row, n_blocks * blk)
    return block_expert, block_nvalid, slot_tok, slot_dst, n_active


def _moe_experts(x, route, counts_row, w_gate, w_up, w_down, layer):
    n, d = x.shape
    n_exp, ff = w_gate.shape[1], w_gate.shape[3]
    blk = MOE_BLOCK
    be, nv, slot_tok, slot_dst, nact = _moe_dispatch(route, counts_row, n_exp, blk)
    n_blocks = be.shape[0]
    grid_spec = pltpu.PrefetchScalarGridSpec(
        num_scalar_prefetch=5,
        grid=(n_blocks,),
        in_specs=[pl.BlockSpec(memory_space=pl.ANY),
                  pl.BlockSpec((None, None, d, ff), lambda i, be, *_: (layer, be[i], 0, 0)),
                  pl.BlockSpec((None, None, d, ff), lambda i, be, *_: (layer, be[i], 0, 0)),
                  pl.BlockSpec((None, None, ff, d), lambda i, be, *_: (layer, be[i], 0, 0))],
        out_specs=pl.BlockSpec(memory_space=pl.ANY),
        scratch_shapes=[pltpu.VMEM((2, blk, d), F32),
                        pltpu.VMEM((2, blk, d), F32),
                        pltpu.VMEM((d, ff), BF16), pltpu.VMEM((d, ff), BF16), pltpu.VMEM((ff, d), BF16),
                        pltpu.SemaphoreType.DMA((2,)), pltpu.SemaphoreType.DMA((2,))],
    )
    y = pl.pallas_call(
        _expert_kernel,
        out_shape=jax.ShapeDtypeStruct((2 * n, d), F32),
        grid_spec=grid_spec,
        compiler_params=_cparams("arbitrary"),
        name="moe_experts",
    )(be, nv, slot_tok, slot_dst, nact, x, w_gate, w_up, w_down)
    return y.reshape(2, n, d)


def _mixer_norm_moe(x, h, ln_g, ln_b, layer, alpha, rg_w, rg_b, re_w, re_b, w_gate, w_up, w_down):
    d = x.shape[1]
    n_r = MOE_GROUPS + MOE_GROUPS * MOE_EPG
    rw = jnp.zeros((d, LANES), F32).at[:, :MOE_GROUPS].set(rg_w[layer]).at[:, MOE_GROUPS:n_r].set(re_w[layer])
    rb = jnp.zeros((1, LANES), F32).at[0, :MOE_GROUPS].set(rg_b[layer]).at[0, MOE_GROUPS:n_r].set(re_b[layer])
    x1, route, counts_row = _ln_mixer(x, h, ln_g, ln_b, layer, 0, alpha, rw, rb)
    y2 = _moe_experts(x1, route, counts_row, w_gate, w_up, w_down, layer)
    return _ln_moe(x1, y2, route, ln_g, ln_b, layer, 1, alpha)


def _gdn_gate_kernel(nh, ab_ref, alog_ref, dtb_ref, gcb_ref, gct_ref):
    ab = ab_ref[...]
    tt = ab.shape[0]
    lane = lax.broadcasted_iota(I32, ab.shape, 1)
    pos = lax.broadcasted_iota(I32, ab.shape, 0) % GDN_CHUNK
    g = -jnp.exp(alog_ref[...]) * jax.nn.softplus(ab + dtb_ref[...])
    s = 1
    while s < GDN_CHUNK:
        g = g + jnp.where(pos >= s, pltpu.roll(g, s, 0), 0.0)
        s *= 2
    out = jnp.where(lane < nh, g, jnp.where(lane < 2 * nh, jax.nn.sigmoid(ab), 0.0))
    gcb_ref[...] = out
    for j in range(tt // LANES):
        gct_ref[j] = out[j * LANES:(j + 1) * LANES, :].T


def _gdn_gates(ab, alog_p, dtb_p, nh, tt=512):
    n = ab.shape[0]
    tt = _tile(n, tt, LANES)
    return pl.pallas_call(
        functools.partial(_gdn_gate_kernel, nh),
        out_shape=(jax.ShapeDtypeStruct((n, LANES), F32),
                   jax.ShapeDtypeStruct((n // LANES, LANES, LANES), F32)),
        grid=(n // tt,),
        in_specs=[pl.BlockSpec((tt, LANES), lambda i: (i, 0)),
                  pl.BlockSpec((1, LANES), lambda i: (0, 0)),
                  pl.BlockSpec((1, LANES), lambda i: (0, 0))],
        out_specs=(pl.BlockSpec((tt, LANES), lambda i: (i, 0)),
                   pl.BlockSpec((tt // LANES, LANES, LANES), lambda i: (i, 0, 0))),
        compiler_params=_cparams("parallel"),
        name="gdn_gates",
    )(ab, alog_p, dtb_p)


def _gdn_conv_kernel(hpb, ncb, x_ref, w_ref, o_ref, carry_ref):
    t = pl.program_id(1)
    j = pl.program_id(2)
    x = x_ref[...]
    tt, tc = x.shape

    @pl.when(t == 0)
    def _():
        carry_ref[j] = jnp.zeros((8, tc), F32)

    prev = jnp.tile(carry_ref[j], (tt // 8, 1))
    carry_ref[j] = x[tt - 8:tt, :]
    row = lax.broadcasted_iota(I32, x.shape, 0)
    acc = x * w_ref[GDN_CONV - 1:GDN_CONV, :]
    for s in range(1, GDN_CONV):
        xs = jnp.where(row < s, pltpu.roll(prev, s, 0), pltpu.roll(x, s, 0))
        acc = acc + xs * w_ref[GDN_CONV - 1 - s:GDN_CONV - s, :]
    y = jax.nn.silu(acc)
    kind = j // (ncb // 3)
    post = jnp.where(kind == 0, GDN_HEAD_DIM ** -0.5, 1.0)
    for hh in range(hpb):
        yh = y[:, hh * LANES:(hh + 1) * LANES]
        inv = lax.rsqrt(jnp.sum(yh * yh, axis=-1, keepdims=True) + RMS_EPS)
        o_ref[hh] = yh * (jnp.where(kind == 2, 1.0, inv) * post)


def _gdn_conv(proj, conv_w, layer, bsz, seq, nh, tt=256):
    hpb = _tile(nh, 8)
    tc = hpb * LANES
    ncb = 3 * nh // hpb
    hb = nh // hpb
    tt = _tile(seq, tt, 8)
    nt = seq // tt
    return pl.pallas_call(
        functools.partial(_gdn_conv_kernel, hpb, ncb),
        out_shape=jax.ShapeDtypeStruct((3, bsz, nh, seq, LANES), F32),
        grid=(bsz, nt, ncb),
        in_specs=[pl.BlockSpec((tt, tc), lambda b, t, j: (b * nt + t, j)),
                  pl.BlockSpec((None, GDN_CONV, tc), lambda b, t, j: (layer, 0, j))],
        out_specs=pl.BlockSpec((None, None, hpb, tt, LANES), lambda b, t, j: (j // hb, b, j % hb, t, 0)),
        scratch_shapes=[pltpu.VMEM((ncb, 8, tc), F32)],
        compiler_params=_cparams("arbitrary", "arbitrary", "arbitrary"),
        name="gdn_conv",
    )(proj, conv_w)


def _head_column(tile, lane, idx):
    return jnp.sum(jnp.where(lane == idx, tile, 0.0), axis=-1, keepdims=True)


def _gdn_local_kernel(nh, q_ref, k_ref, gcb_ref, gct_ref, l_ref, a_ref):
    h = pl.program_id(1)
    c = GDN_CHUNK
    lane = lax.broadcasted_iota(I32, (LANES, LANES), 1)
    row = lax.broadcasted_iota(I32, (LANES, LANES), 0)
    lower = ((row // c) == (lane // c)) & (lane <= row)
    for s in range(q_ref.shape[0] // LANES):
        rows = slice(s * LANES, (s + 1) * LANES)
        q = q_ref[rows, :]
        k = k_ref[rows, :]
        gcb = gcb_ref[rows, :]
        gcol = _head_column(gcb, lane, h)
        bcol = _head_column(gcb, lane, nh + h)
        grow = gct_ref[s, pl.ds(h, 1), :]
        decay = jnp.where(lower, jnp.exp(jnp.where(lower, gcol - grow, 0.0)), 0.0)
        kbf = k.astype(BF16)
        kk = _dot_nt((k * bcol).astype(BF16), kbf)
        qk = _dot_nt(q.astype(BF16), kbf)
        l_ref[s] = jnp.where(lane < row, kk * decay, 0.0)
        a_ref[s] = (qk * decay).astype(BF16)


def _gdn_local(qkv, gcb, gct, bsz, seq, nh, tt=2048):
    c = LANES
    tt = _tile(seq, tt, LANES)
    nt = seq // tt
    nc = seq // c
    qk_spec = lambda which: pl.BlockSpec((None, None, None, tt, LANES), lambda b, h, t: (which, b, h, t, 0))
    out_spec = pl.BlockSpec((None, None, tt // c, c, c), lambda b, h, t: (b, h, t, 0, 0))
    return pl.pallas_call(
        functools.partial(_gdn_local_kernel, nh),
        out_shape=(jax.ShapeDtypeStruct((bsz, nh, nc, c, c), F32),
                   jax.ShapeDtypeStruct((bsz, nh, nc, c, c), BF16)),
        grid=(bsz, nh, nt),
        in_specs=[qk_spec(0), qk_spec(1),
                  pl.BlockSpec((tt, LANES), lambda b, h, t: (b * nt + t, 0)),
                  pl.BlockSpec((tt // LANES, LANES, LANES), lambda b, h, t: (b * nt + t, 0, 0))],
        out_specs=(out_spec, out_spec),
        compiler_params=_cparams("parallel", "parallel", "parallel"),
        name="gdn_local",
    )(qkv, qkv, gcb, gct)


def _gdn_solve_kernel(l_ref, t_ref, lt_s, tt_s):
    c = GDN_CHUNK
    pr = 2 * c
    npb = l_ref.shape[0] // pr
    for i in range(c):
        both = l_ref[pl.ds(i, npb, stride=pr), :] + l_ref[pl.ds(c + i, npb, stride=pr), :]
        bt = both.T
        lt_s[0, i] = bt[:c, :]
        lt_s[1, i] = bt[c:, :]

    halves = (0, 1)
    for g in halves:
        tt_s[g, 0] = jnp.where(lax.broadcasted_iota(I32, (c, npb), 0) == 0, 1.0, 0.0)
    for ib in range(c // 8):
        ext = 8 * (ib + 1)
        sub = lax.broadcasted_iota(I32, (ext, npb), 0)

        def row_body(ii, carry, ib=ib, ext=ext, sub=sub):
            i = ib * 8 + ii
            accs = [jnp.where(sub == i, 1.0, 0.0) for _ in halves]
            for mb in range(ib):
                em = 8 * (mb + 1)
                for g in halves:
                    t = sum(lt_s[g, i, m:m + 1, :] * tt_s[g, m, 0:em, :] for m in range(8 * mb, em))
                    accs[g] = jnp.concatenate([accs[g][:em] - t, accs[g][em:]], axis=0)

            def m_body(mm, a):
                m = ib * 8 + mm
                return tuple(a[g] - lt_s[g, i, pl.ds(m, 1), :] * tt_s[g, m, 0:ext, :] for g in halves)
            accs = lax.fori_loop(0, ii, m_body, tuple(accs))
            for g in halves:
                tt_s[g, i, 0:ext, :] = accs[g]
                if ext < c:
                    tt_s[g, i, ext:c, :] = jnp.zeros((c - ext, npb), F32)
            return carry

        lax.fori_loop(1 if ib == 0 else 0, 8, row_body, 0)

    lane = lax.broadcasted_iota(I32, (npb, pr), 1)
    for i in range(c):
        rows = jnp.concatenate([tt_s[0, i], tt_s[1, i]], axis=0).T
        t_ref[pl.ds(i, npb, stride=pr), :] = jnp.where(lane < c, rows, 0.0)
        t_ref[pl.ds(c + i, npb, stride=pr), :] = jnp.where(lane >= c, rows, 0.0)


def _gdn_solve(lpair, pairs_per_step=128):
    c = GDN_CHUNK
    pr = 2 * c
    n_pairs = lpair.shape[0] // pr
    npb = _tile(n_pairs, pairs_per_step)
    spec = pl.BlockSpec((npb * pr, pr), lambda i: (i, 0))
    return pl.pallas_call(
        _gdn_solve_kernel,
        out_shape=jax.ShapeDtypeStruct(lpair.shape, F32),
        grid=(n_pairs // npb,),
        in_specs=[spec],
        out_specs=spec,
        scratch_shapes=[pltpu.VMEM((2, c, c, npb), F32), pltpu.VMEM((2, c, c, npb), F32)],
        compiler_params=_cparams("parallel"),
        name="gdn_solve",
    )(lpair)


def _gdn_recur_kernel(nh, q_ref, k_ref, v_ref, z_ref, gcb_ref, tm_ref, at_ref, ng_ref, o_ref, st_ref):
    hblk = pl.program_id(1)
    c = GDN_CHUNK
    pr = 2 * c
    hb, tt, dv = v_ref.shape
    lane = lax.broadcasted_iota(I32, (c, LANES), 1)

    @pl.when(pl.program_id(2) == 0)
    def _():
        st_ref[...] = jnp.zeros(st_ref.shape, F32)

    def body(pi, carry):
        for half in range(2):
            rows = pl.ds(pl.multiple_of(pi * pr, pr) + half * c, c)
            blk = slice(half * c, (half + 1) * c)
            gcb = gcb_ref[rows, :]
            for j in range(hb):
                h = hblk * hb + j
                q = q_ref[j, rows, :]
                k = k_ref[j, rows, :]
                v = v_ref[j, rows, :]
                gcol = _head_column(gcb, lane, h)
                bcol = _head_column(gcb, lane, nh + h)
                glast = gcol[c - 1:c, :]
                eg = jnp.exp(gcol)
                kb = k * bcol
                sol = _dot3(tm_ref[j, pi, blk, blk], jnp.concatenate([v * bcol, kb * eg], axis=-1))
                u = sol[:, :dv]
                w = sol[:, dv:]
                state = st_ref[j]
                sb = state.astype(BF16)
                v_new = u - _dot(w.astype(BF16), sb)
                vnb = v_new.astype(BF16)
                o = _dot((q * eg).astype(BF16), sb) + _dot(at_ref[j, pi, blk, blk], vnb)
                kd = k * jnp.exp(glast - gcol)
                st_ref[j] = state * jnp.exp(glast) + _dot_tn(kd.astype(BF16), vnb)
                o = o * lax.rsqrt(jnp.mean(o * o, axis=-1, keepdims=True) + RMS_EPS) * ng_ref[...]
                o = o * jax.nn.silu(z_ref[rows, j * dv:(j + 1) * dv])
                o_ref[rows, j * dv:(j + 1) * dv] = o.astype(o_ref.dtype)
        return carry

    lax.fori_loop(0, tt // pr, body, 0)


def _gdn_recur(qkv, proj, gcb, tm, at, norm_g, layer, bsz, seq, nh, hb=16, tt=256):
    c = 2 * GDN_CHUNK
    hb = _tile(nh, hb)
    tt = _tile(seq, tt, c)
    nt = seq // tt
    head = lambda which: pl.BlockSpec((None, None, hb, tt, LANES), lambda b, h, t: (which, b, h, t, 0))
    mat = pl.BlockSpec((None, hb, tt // c, c, c), lambda b, h, t: (b, h, t, 0, 0))
    return pl.pallas_call(
        functools.partial(_gdn_recur_kernel, nh),
        out_shape=jax.ShapeDtypeStruct((bsz * seq, nh * LANES), BF16),
        grid=(bsz, nh // hb, nt),
        in_specs=[head(0), head(1), head(2),
                  pl.BlockSpec((tt, hb * LANES), lambda b, h, t: (b * nt + t, 3 * nh // hb + h)),
                  pl.BlockSpec((tt, LANES), lambda b, h, t: (b * nt + t, 0)),
                  mat, mat,
                  pl.BlockSpec((None, 1, LANES), lambda b, h, t: (layer, 0, 0))],
        out_specs=pl.BlockSpec((tt, hb * LANES), lambda b, h, t: (b * nt + t, h)),
        scratch_shapes=[pltpu.VMEM((hb, GDN_HEAD_DIM, LANES), F32)],
        compiler_params=_cparams("parallel", "parallel", "arbitrary"),
        name="gdn_recur",
    )(qkv, qkv, qkv, proj, gcb, tm, at, norm_g.reshape(-1, 1, LANES))


def _pad_lanes(v):
    return jnp.zeros((1, LANES), F32).at[0, :v.shape[0]].set(v)


def _gdn_mixer(x_bf, bsz, seq, w_in, conv_w, a_log, dt_bias, norm_g, w_out, layer):
    n, d = x_bf.shape
    nh = d // GDN_HEAD_DIM
    c = GDN_CHUNK
    proj = _matmul(x_bf, w_in, layer, 4 * d, tn=1024, name="gdn_in")
    w_ab = jnp.zeros((1, d, LANES), F32).at[0, :, :2 * nh].set(w_in[layer, :, 4 * d:])
    ab = _matmul(x_bf, w_ab, 0, LANES, name="gdn_in_ab")
    gcb, gct = _gdn_gates(ab, _pad_lanes(a_log[layer]), _pad_lanes(dt_bias[layer]), nh)
    qkv = _gdn_conv(proj, conv_w, layer, bsz, seq, nh)
    lpair, at = _gdn_local(qkv, gcb, gct, bsz, seq, nh)
    tm = _gdn_solve(lpair.reshape(-1, 2 * c)).reshape(lpair.shape)
    og = _gdn_recur(qkv, proj, gcb, tm, at, norm_g, layer, bsz, seq, nh)
    return _matmul(og, w_out, layer, d, name="gdn_out")


def _s5_matrices(b_re, b_im, c_re, c_im, a_re, a_im, log_dt):
    lc = SSM_CHUNK
    g, p, nch = b_re.shape
    dt = jnp.exp(log_dt)[:, None]
    mag = jnp.exp(a_re * dt)
    ang = a_im * dt
    lb_re = mag * jnp.cos(ang)
    lb_im = mag * jnp.sin(ang)
    den = jnp.square(a_re) + jnp.square(a_im)
    f_re = ((lb_re - 1.0) * a_re + lb_im * a_im) / den
    f_im = (lb_im * a_re - (lb_re - 1.0) * a_im) / den
    bb_re = f_re[..., None] * b_re - f_im[..., None] * b_im
    bb_im = f_re[..., None] * b_im + f_im[..., None] * b_re
    dd = jnp.arange(lc + 1, dtype=F32)[:, None, None]
    pw_mag = jnp.exp(dd * (a_re * dt))
    pw_re = pw_mag * jnp.cos(dd * ang)
    pw_im = pw_mag * jnp.sin(dd * ang)
    gpb = LANES // nch
    nb = g // gpb
    wide = gpb * p
    rows = lambda a: a.reshape(nb, LANES, p)
    lanes = lambda a: jnp.transpose(a.reshape(lc + 1, nb, wide), (1, 0, 2))
    bt_re = rows(jnp.transpose(bb_re, (0, 2, 1)))
    bt_im = rows(jnp.transpose(bb_im, (0, 2, 1)))
    pw_re, pw_im = lanes(pw_re), lanes(pw_im)
    width = lc * LANES
    per_block = lambda a: pl.BlockSpec((None,) + a.shape[1:], lambda i: (i, 0, 0))
    ins = (rows(c_re), rows(c_im), bt_re, bt_im, pw_re, pw_im)
    out = jax.ShapeDtypeStruct((nb, width, width), BF16)
    assert 2 * wide == width
    mcat, qcat, pcat = pl.pallas_call(
        functools.partial(_s5_operator_kernel, lc),
        out_shape=(out, out, out),
        grid=(nb,),
        in_specs=[per_block(a) for a in ins],
        out_specs=(per_block(out), per_block(out), per_block(out)),
        compiler_params=_cparams("parallel"),
        name="s5_operators",
    )(*ins)
    a_pow = jnp.stack([pw_re[:, lc], pw_im[:, lc]], axis=1)
    return mcat, qcat, pcat, a_pow


def _dot3_nt(a, b):
    ah, al = _split_bf16(a)
    bh, bl = _split_bf16(b)
    return _dot_nt(ah, bh) + (_dot_nt(ah, bl) + _dot_nt(al, bh))


def _s5_operator_kernel(lc, c_re_ref, c_im_ref, b_re_ref, b_im_ref, pw_re_ref, pw_im_ref, m_ref, q_ref, p_ref):
    p = c_re_ref.shape[1]
    wide = pw_re_ref.shape[1]
    row = lax.broadcasted_iota(I32, (LANES, wide), 0)
    lane = lax.broadcasted_iota(I32, (LANES, wide), 1)
    own = (row // SSM_GROUP) == (lane // p)

    def embed(ref):
        return jnp.where(own, jnp.concatenate([ref[...]] * (wide // p), axis=1), 0.0)

    c_re, c_im, b_re, b_im = embed(c_re_ref), embed(c_im_ref), embed(b_re_ref), embed(b_im_ref)
    taps = []
    for d in range(lc + 1):
        pr = pw_re_ref[d:d + 1, :]
        pi = pw_im_ref[d:d + 1, :]
        cp_re = c_re * pr - c_im * pi
        cp_im = c_re * pi + c_im * pr
        if d < lc:
            taps.append(_dot3_nt(b_re, cp_re) - _dot3_nt(b_im, cp_im))
            s = lc - 1 - d
            q_ref[s * LANES:(s + 1) * LANES, :wide] = (b_re * pr - b_im * pi).astype(BF16)
            q_ref[s * LANES:(s + 1) * LANES, wide:] = (b_re * pi + b_im * pr).astype(BF16)
        if d > 0:
            t = d - 1
            p_ref[:wide, t * LANES:(t + 1) * LANES] = cp_re.T.astype(BF16)
            p_ref[wide:, t * LANES:(t + 1) * LANES] = (-cp_im).T.astype(BF16)
    zero = jnp.zeros((LANES, LANES), BF16)
    for s in range(lc):
        for t in range(lc):
            m_ref[s * LANES:(s + 1) * LANES, t * LANES:(t + 1) * LANES] = (
                taps[t - s].astype(BF16) if t >= s else zero)


def _s5_scan_kernel(lc, nck, u_ref, m_ref, q_ref, p_ref, a_ref, d_ref, v_ref, x_s, h_s):
    r = u_ref.shape[0] // lc
    bl = r // nck
    nhalf = x_s.shape[0] // 2
    step = lambda s: pl.ds(s, r, stride=lc)
    ucat = jnp.concatenate([u_ref[step(s), :] for s in range(lc)], axis=-1).astype(BF16)
    x = _dot(ucat, q_ref[...])
    for j in range(2 * nhalf):
        x_s[j] = x[:, j * LANES:(j + 1) * LANES]
    ar = [a_ref[0:1, j * LANES:(j + 1) * LANES] for j in range(nhalf)]
    ai = [a_ref[1:2, j * LANES:(j + 1) * LANES] for j in range(nhalf)]

    def body(ci, carry):
        rows = pl.ds(ci, bl, stride=nck)
        out = []
        for j in range(nhalf):
            sr, si = carry[j], carry[nhalf + j]
            h_s[j, rows, :] = sr
            h_s[nhalf + j, rows, :] = si
            out.append((ar[j] * sr - ai[j] * si + x_s[j, rows, :],
                        ar[j] * si + ai[j] * sr + x_s[nhalf + j, rows, :]))
        return tuple(o[0] for o in out) + tuple(o[1] for o in out)

    lax.fori_loop(0, nck, body, tuple(jnp.zeros((bl, LANES), F32) for _ in range(2 * nhalf)))
    h = jnp.concatenate([h_s[j] for j in range(2 * nhalf)], axis=-1).astype(BF16)
    y = _dot(ucat, m_ref[...]) + _dot(h, p_ref[...])
    for s in range(lc):
        v_ref[step(s), :] = jax.nn.gelu(y[:, s * LANES:(s + 1) * LANES] + d_ref[...] * u_ref[step(s), :])


def _s5_scan(u, mats, d_skip, layer, bsz, seq):
    n, w = u.shape
    lc = SSM_CHUNK
    nck = seq // lc
    mcat, qcat, pcat, a_pow = mats
    nb = mcat.shape[0]
    bl = _tile(bsz, 4)
    rows = bl * seq
    r = rows // lc
    per_block = lambda a: pl.BlockSpec((None,) + a.shape[1:], lambda i, j: (i,) + (0,) * (a.ndim - 1))
    io = pl.BlockSpec((rows, LANES), lambda i, j: (j, i))
    return pl.pallas_call(
        functools.partial(_s5_scan_kernel, lc, nck),
        out_shape=jax.ShapeDtypeStruct((n, w), F32),
        grid=(nb, bsz // bl),
        in_specs=[io, per_block(mcat), per_block(qcat), per_block(pcat), per_block(a_pow),
                  pl.BlockSpec((None, 1, LANES), lambda i, j: (layer, 0, i))],
        out_specs=io,
        scratch_shapes=[pltpu.VMEM((qcat.shape[2] // LANES, r, LANES), F32)] * 2,
        compiler_params=_cparams("parallel", "parallel"),
        name="s5_scan",
    )(u, mcat, qcat, pcat, a_pow, d_skip.reshape(-1, 1, w))


def _s5_glu_kernel(vrow_ref, v_ref, w_ref, b_ref, o_ref, wbf_ref):
    @pl.when(pl.program_id(1) == 0)
    def _():
        wbf_ref[...] = w_ref[...].astype(BF16)

    z = _dot(vrow_ref[...].astype(BF16), wbf_ref[...]) + b_ref[...]
    o_ref[...] = (v_ref[...] * jax.nn.sigmoid(z)).astype(o_ref.dtype)


def _s5_glu(v, w_glu, b_glu, layer, tm=1024, tn=512):
    n, w = v.shape
    tm = _tile(n, tm)
    tn = _tile(w, tn, LANES)
    return pl.pallas_call(
        _s5_glu_kernel,
        out_shape=jax.ShapeDtypeStruct((n, w), BF16),
        grid=(w // tn, n // tm),
        in_specs=[pl.BlockSpec((tm, w), lambda j, i: (i, 0)),
                  pl.BlockSpec((tm, tn), lambda j, i: (i, j)),
                  pl.BlockSpec((None, w, tn), lambda j, i: (layer, 0, j)),
                  pl.BlockSpec((None, 1, tn), lambda j, i: (layer, 0, j))],
        out_specs=pl.BlockSpec((tm, tn), lambda j, i: (i, j)),
        scratch_shapes=[pltpu.VMEM((w, tn), BF16)],
        compiler_params=_cparams("arbitrary", "arbitrary"),
        name="s5_glu",
    )(v, v, w_glu, b_glu.reshape(-1, 1, w))


def _s5_mixer(x_bf, bsz, seq, w_in, b_re, b_im, c_re, c_im, a_re, a_im, log_dt, d_skip,
              w_glu, b_glu, w_out, layer):
    n, d = x_bf.shape
    w = w_in.shape[2]
    u = _matmul(x_bf, w_in, layer, w, name="s5_in")
    mats = _s5_matrices(b_re[layer], b_im[layer], c_re[layer], c_im[layer], a_re[layer], a_im[layer], log_dt[layer])
    v = _s5_scan(u, mats, d_skip, layer, bsz, seq)
    hg = _s5_glu(v, w_glu, b_glu, layer)
    return _matmul(hg, w_out, layer, d, name="s5_out")


INT_MIN = -2 ** 31
SUM_ROWS = 16


def _fold_rows(x, op):
    return functools.reduce(op, [x[r:r + 8, :] for r in range(0, x.shape[0], 8)])


def _dsa_kernel(nsel, q_ref, qi_ref, wi_ref, k_ref, v_ref, ki_ref, o_ref,
                kbf, vext_t, kibf, qis, qs, w_t, slope_s, keys, selm, s_s, acc_s):
    qb = pl.program_id(1)
    blk = Q_BLOCK
    hd = ATT_HEAD_DIM
    nah = q_ref.shape[1] // hd
    nkb = qb + 1

    @pl.when(qb == 0)
    def _():
        kbf[...] = k_ref[...].astype(BF16)
        kibf[...] = ki_ref[...].astype(BF16)
        for j in range(vext_t.shape[0]):
            vext_t[j, :hd, :] = v_ref[j * blk:(j + 1) * blk, :].T.astype(BF16)
            vext_t[j, hd:, :] = jnp.ones((vext_t.shape[1] - hd, blk), BF16)
        for h in range(nah):
            slope_s[:, h * blk:(h + 1) * blk] = jnp.full((8, blk), 2.0 ** (-8.0 * (h + 1) / nah), F32)

    qi = qi_ref[...]
    for h in range(IDX_HEADS):
        qis[h * blk:(h + 1) * blk, :] = qi[:, h * IDX_DIM:(h + 1) * IDX_DIM].astype(BF16)
    q = q_ref[...]
    for h in range(nah):
        qs[h * blk:(h + 1) * blk, :] = q[:, h * hd:(h + 1) * hd].astype(BF16)
    w_t[...] = (wi_ref[...] * (IDX_HEADS ** -0.5 * IDX_DIM ** -0.5)).T

    row = lax.broadcasted_iota(I32, (blk, blk), 0)
    col = lax.broadcasted_iota(I32, (blk, blk), 1)
    tpos = qb * blk + col

    def score_body(kb, carry):
        off = pl.multiple_of(kb * blk, blk)
        dots = _dot_nt(kibf[pl.ds(off, blk), :], qis[...])
        sc = jnp.zeros((blk, blk), F32)
        for h in range(IDX_HEADS):
            sc = sc + jnp.maximum(dots[:, h * blk:(h + 1) * blk], 0.0) * w_t[h:h + 1, :]
        sc = jnp.where(sc == 0.0, 0.0, sc)
        bits = pltpu.bitcast(sc, I32)
        key = jnp.where(bits < 0, bits ^ 0x7FFFFFFF, bits)
        keys[kb] = jnp.where(off + row <= tpos, key, INT_MIN)
        return carry

    lax.fori_loop(0, nkb, score_body, 0)

    def count_ge(cands):
        def body(kb, accs):
            key = keys[kb]
            return tuple(acc + jnp.where(key >= cand, 1.0, 0.0) for acc, cand in zip(accs, cands))
        accs = lax.fori_loop(0, nkb, body, tuple(jnp.zeros((blk, blk), F32) for _ in cands))
        return [jnp.sum(_fold_rows(acc, jnp.add), axis=0, keepdims=True) for acc in accs]

    def bit_body(t, th):
        cand = th + jnp.left_shift(jnp.int32(1), 31 - t)
        return jnp.where(count_ge([cand])[0] >= nsel, cand, th)

    th = lax.fori_loop(0, 32, bit_body, jnp.full((1, blk), INT_MIN, I32))
    n_gt, n_ge = count_ge([th + 1, th])
    overflow = jnp.where((n_ge > nsel) & (th > INT_MIN), 1.0, 0.0)
    has_ties = jnp.max(overflow) > 0.0

    @pl.when(jnp.logical_not(has_ties))
    def _():
        floor = jnp.maximum(th, INT_MIN + 1)

        def body(kb, carry):
            selm[kb] = jnp.where(keys[kb] >= floor, 1.0, 0.0)
            return carry
        lax.fori_loop(0, nkb, body, 0)

    @pl.when(has_ties)
    def _():
        need = nsel - n_gt
        tri = jnp.where(col <= row, 1.0, 0.0).astype(BF16)

        def body(kb, run):
            key = keys[kb]
            eq = jnp.where(key == th, 1.0, 0.0)
            rank = run + _dot(tri, eq.astype(BF16))
            take = (key > th) | ((key == th) & (rank <= need))
            selm[kb] = jnp.where(take & (key > INT_MIN), 1.0, 0.0)
            return run + jnp.sum(eq, axis=0, keepdims=True)
        lax.fori_loop(0, nkb, body, jnp.zeros((1, blk), F32))

    kcol = lax.broadcasted_iota(I32, (blk, 1), 0)

    def max_body(kb, m8):
        off = pl.multiple_of(kb * blk, blk)
        kpos = (off + kcol).astype(F32)
        keep = jnp.tile(selm[kb], (1, nah)) > 0.5
        s = _dot_nt(kbf[pl.ds(off, blk), :], qs[...]) * (hd ** -0.5) + kpos * slope_s[0:1, :]
        s = jnp.where(keep, s, NEG_INF)
        s_s[kb] = s
        return jnp.maximum(m8, _fold_rows(s, jnp.maximum))

    m8 = lax.fori_loop(0, nkb, max_body, jnp.full((8, nah * blk), NEG_INF, F32))
    m = jnp.max(m8, axis=0, keepdims=True)
    acc_s[...] = jnp.zeros(acc_s.shape, F32)

    def acc_body(kb, carry):
        p = jnp.exp(s_s[kb] - m).astype(BF16)
        acc_s[...] += _dot(vext_t[kb], p)
        return carry

    lax.fori_loop(0, nkb, acc_body, 0)
    out_t = acc_s[:hd, :] / acc_s[hd:hd + 1, :]
    for h in range(nah):
        o_ref[:, h * hd:(h + 1) * hd] = out_t[:, h * blk:(h + 1) * blk].T.astype(o_ref.dtype)


def _dsa_attention(proj, bsz, seq, d):
    blk = Q_BLOCK
    nq = seq // blk
    nah = d // ATT_HEAD_DIM
    iw = IDX_HEADS * IDX_DIM
    assert iw % d == 0
    kcol = (iw + d) // LANES
    nsel = float(min(TOPK_MAX, seq // 4))
    rowblk = lambda width, cidx: pl.BlockSpec((blk, width), lambda b, i: (b * nq + i, cidx))
    seqblk = lambda cidx: pl.BlockSpec((seq, LANES), lambda b, i: (b, cidx))
    return pl.pallas_call(
        functools.partial(_dsa_kernel, nsel),
        out_shape=jax.ShapeDtypeStruct((bsz * seq, d), BF16),
        grid=(bsz, nq),
        in_specs=[rowblk(d, iw // d), rowblk(iw, 0), rowblk(LANES, kcol + 3),
                  seqblk(kcol), seqblk(kcol + 1), seqblk(kcol + 2)],
        out_specs=pl.BlockSpec((blk, d), lambda b, i: (b * nq + i, 0)),
        scratch_shapes=[pltpu.VMEM((seq, ATT_HEAD_DIM), BF16),
                        pltpu.VMEM((nq, ATT_HEAD_DIM + SUM_ROWS, blk), BF16),
                        pltpu.VMEM((seq, IDX_DIM), BF16),
                        pltpu.VMEM((IDX_HEADS * blk, IDX_DIM), BF16),
                        pltpu.VMEM((nah * blk, ATT_HEAD_DIM), BF16),
                        pltpu.VMEM((LANES, blk), F32),
                        pltpu.VMEM((8, nah * blk), F32),
                        pltpu.VMEM((nq, blk, blk), I32),
                        pltpu.VMEM((nq, blk, blk), F32),
                        pltpu.VMEM((nq, blk, nah * blk), F32),
                        pltpu.VMEM((ATT_HEAD_DIM + SUM_ROWS, nah * blk), F32)],
        compiler_params=_cparams("parallel", "arbitrary"),
        name="dsa_attention",
    )(proj, proj, proj, proj, proj, proj)


def _dsa_mixer(x_bf, bsz, seq, w_in, w_out, layer):
    n, d = x_bf.shape
    hd = ATT_HEAD_DIM
    iw = IDX_HEADS * IDX_DIM
    w = w_in[layer]
    o_k, o_v, o_qi, o_ki, o_wi = d, d + hd, d + 2 * hd, d + 2 * hd + iw, d + 2 * hd + iw + IDX_DIM
    cols = [w[:, o_qi:o_ki], w[:, :o_k], w[:, o_k:o_v], w[:, o_v:o_qi], w[:, o_ki:o_wi], w[:, o_wi:]]
    width = sum(c.shape[1] for c in cols)
    pad = -(-width // (4 * LANES)) * (4 * LANES) - width
    w_re = jnp.concatenate(cols + [jnp.zeros((d, pad), F32)], axis=1)[None]
    proj = _matmul(x_bf, w_re, 0, width + pad, name="dsa_in")
    o = _dsa_attention(proj, bsz, seq, d)
    return _matmul(o, w_out, layer, d, name="dsa_out")


def kernel(x, ln_g, ln_b, moe_rg_w, moe_rg_b, moe_re_w, moe_re_b, moe_w_gate, moe_w_up, moe_w_down,
           gdn_w_in, gdn_conv_w, gdn_a_log, gdn_dt_bias, gdn_norm_g, gdn_w_out,
           ssm_w_in, ssm_b_re, ssm_b_im, ssm_c_re, ssm_c_im, ssm_a_re, ssm_a_im, ssm_log_dt,
           ssm_d, ssm_w_glu, ssm_b_glu, ssm_w_out, dsa_w_in, dsa_w_out):
    bsz, seq, d = x.shape
    depth = ln_g.shape[0]
    alpha = (2.0 * depth) ** 0.25
    xf = x.reshape(bsz * seq, d)
    xb = xf.astype(BF16)
    counts = [0, 0, 0]
    for layer in range(depth):
        kind = layer % 3
        i = counts[kind]
        counts[kind] += 1
        if kind == 0:
            h = _gdn_mixer(xb, bsz, seq, gdn_w_in, gdn_conv_w, gdn_a_log, gdn_dt_bias, gdn_norm_g, gdn_w_out, i)
        elif kind == 1:
            h = _s5_mixer(xb, bsz, seq, ssm_w_in, ssm_b_re, ssm_b_im, ssm_c_re, ssm_c_im, ssm_a_re, ssm_a_im,
                          ssm_log_dt, ssm_d, ssm_w_glu, ssm_b_glu, ssm_w_out, i)
        else:
            h = _dsa_mixer(xb, bsz, seq, dsa_w_in, dsa_w_out, i)
        xf, xb = _mixer_norm_moe(xf, h, ln_g, ln_b, layer, alpha, moe_rg_w, moe_rg_b, moe_re_w, moe_re_b,
                                 moe_w_gate, moe_w_up, moe_w_down)
    return xf.reshape(bsz, seq, d)
```

```python
import functools
import math

import jax
import jax.numpy as jnp
from jax import lax
from jax.experimental import pallas as pl
from jax.experimental.pallas import tpu as pltpu

F32 = jnp.float32
BF16 = jnp.bfloat16
I32 = jnp.int32

LANES = 128
VMEM_LIMIT = 56 * 1024 * 1024

LN_EPS = 1e-5
RMS_EPS = 1e-6
NEG_INF = -1e30

GDN_HEAD_DIM = 128
GDN_CONV = 4
GDN_CHUNK = 64
SSM_GROUP = 16
SSM_STATE = 64
SSM_CHUNK = 8
ATT_HEAD_DIM = 128
IDX_HEADS = 16
IDX_DIM = 128
TOPK_MAX = 256
Q_BLOCK = 128
MOE_GROUPS = 4
MOE_EPG = 8
MOE_BLOCK = 256


def _cparams(*sem):
    return pltpu.CompilerParams(dimension_semantics=sem, vmem_limit_bytes=VMEM_LIMIT)


def _tile(n, pref, mult=1):
    t = min(n, pref) // mult * mult
    while t > mult and n % t:
        t -= mult
    assert t > 0 and n % t == 0, (n, pref, mult)
    return t


def _dot(a, b):
    return jnp.dot(a, b, preferred_element_type=F32)


def _dot_nt(a, b):
    return lax.dot_general(a, b, (((1,), (1,)), ((), ())), preferred_element_type=F32)


def _dot_tn(a, b):
    return lax.dot_general(a, b, (((0,), (0,)), ((), ())), preferred_element_type=F32)


def _split_bf16(x):
    hi = x.astype(BF16)
    lo = (x - hi.astype(F32)).astype(BF16)
    return hi, lo


def _dot3(a, b):
    ah, al = _split_bf16(a)
    bh, bl = _split_bf16(b)
    return _dot(ah, bh) + (_dot(ah, bl) + _dot(al, bh))


def _mm_kernel(x_ref, w_ref, o_ref, wbf_ref):
    @pl.when(pl.program_id(1) == 0)
    def _():
        wbf_ref[...] = w_ref[...].astype(BF16)

    o_ref[...] = _dot(x_ref[...], wbf_ref[...]).astype(o_ref.dtype)


def _matmul(x, w, layer, n_cols, *, col0=0, out_dtype=F32, tm=1024, tn=1024, name="matmul"):
    m, k = x.shape
    tm = _tile(m, tm)
    tn = _tile(n_cols, tn, LANES)
    assert col0 % tn == 0
    jb = col0 // tn
    return pl.pallas_call(
        _mm_kernel,
        out_shape=jax.ShapeDtypeStruct((m, n_cols), out_dtype),
        grid=(n_cols // tn, m // tm),
        in_specs=[pl.BlockSpec((tm, k), lambda j, i: (i, 0)),
                  pl.BlockSpec((None, k, tn), lambda j, i: (layer, 0, j + jb))],
        out_specs=pl.BlockSpec((tm, tn), lambda j, i: (i, j)),
        scratch_shapes=[pltpu.VMEM((k, tn), BF16)],
        compiler_params=_cparams("arbitrary", "arbitrary"),
        name=name,
    )(x, w)


def _layer_norm_rows(v, g, b):
    mu = jnp.mean(v, axis=-1, keepdims=True)
    c = v - mu
    var = jnp.mean(c * c, axis=-1, keepdims=True)
    return c * lax.rsqrt(var + LN_EPS) * g + b


def _ln_mixer_kernel(alpha, x_ref, h_ref, g_ref, b_ref, rw_ref, rb_ref, o_ref, route_ref, cnt_ref):
    y = _layer_norm_rows(alpha * x_ref[...] + h_ref[...], g_ref[...], b_ref[...])
    o_ref[...] = y
    route, hit = _route_rows(y, rw_ref[...], rb_ref[...])
    route_ref[...] = route

    @pl.when(pl.program_id(0) == 0)
    def _():
        cnt_ref[...] = jnp.zeros(cnt_ref.shape, F32)

    cnt_ref[...] += jnp.sum(jnp.where(hit, 1.0, 0.0), axis=0, keepdims=True)


def _ln_moe_kernel(alpha, x_ref, y_ref, route_ref, g_ref, b_ref, o_ref, obf_ref):
    h = y_ref[0] * route_ref[:, 2:3] + y_ref[1] * route_ref[:, 3:4]
    y = _layer_norm_rows(alpha * x_ref[...] + h, g_ref[...], b_ref[...])
    o_ref[...] = y
    obf_ref[...] = y.astype(BF16)


def _ln_mixer(x, h, ln_g, ln_b, layer, which, alpha, rw, rb, tm=512):
    n, d = x.shape
    tm = _tile(n, tm)
    row = pl.BlockSpec((tm, d), lambda i: (i, 0))
    par = pl.BlockSpec((None, 1, d), lambda i: (layer * 2 + which, 0, 0))
    lanes = pl.BlockSpec((tm, LANES), lambda i: (i, 0))
    fixed = lambda shape: pl.BlockSpec(shape, lambda i: (0, 0))
    return pl.pallas_call(
        functools.partial(_ln_mixer_kernel, alpha),
        out_shape=(jax.ShapeDtypeStruct((n, d), F32), jax.ShapeDtypeStruct((n, LANES), F32),
                   jax.ShapeDtypeStruct((1, LANES), F32)),
        grid=(n // tm,),
        in_specs=[row, row, par, par, fixed((d, LANES)), fixed((1, LANES))],
        out_specs=(row, lanes, fixed((1, LANES))),
        compiler_params=_cparams("arbitrary"),
        name="ln_mixer",
    )(x, h, ln_g.reshape(-1, 1, d), ln_b.reshape(-1, 1, d), rw, rb)


def _ln_moe(x, y2, route, ln_g, ln_b, layer, which, alpha, tm=512):
    n, d = x.shape
    tm = _tile(n, tm)
    row = pl.BlockSpec((tm, d), lambda i: (i, 0))
    par = pl.BlockSpec((None, 1, d), lambda i: (layer * 2 + which, 0, 0))
    return pl.pallas_call(
        functools.partial(_ln_moe_kernel, alpha),
        out_shape=(jax.ShapeDtypeStruct((n, d), F32), jax.ShapeDtypeStruct((n, d), BF16)),
        grid=(n // tm,),
        in_specs=[row, pl.BlockSpec((2, tm, d), lambda i: (0, i, 0)),
                  pl.BlockSpec((tm, LANES), lambda i: (i, 0)), par, par],
        out_specs=(row, row),
        compiler_params=_cparams("parallel"),
        name="ln_moe",
    )(x, y2, route, ln_g.reshape(-1, 1, d), ln_b.reshape(-1, 1, d))


def _first_lane_where(cond, lane):
    return jnp.min(jnp.where(cond, lane, LANES), axis=-1, keepdims=True)


def _route_rows(x, w, b):
    logits = _dot3(x, w) + b
    lane = lax.broadcasted_iota(I32, logits.shape, 1)
    gmask = lane < MOE_GROUPS
    lg = jnp.where(gmask, logits, NEG_INF)
    eg = jnp.where(gmask, jnp.exp(lg - jnp.max(lg, axis=-1, keepdims=True)), 0.0)
    pg = eg / jnp.sum(eg, axis=-1, keepdims=True)
    p_g = jnp.max(pg, axis=-1, keepdims=True)
    g_idx = _first_lane_where(gmask & (pg == p_g), lane)
    lo = MOE_GROUPS + g_idx * MOE_EPG
    emask = (lane >= lo) & (lane < lo + MOE_EPG)
    le = jnp.where(emask, logits, NEG_INF)
    ee = jnp.where(emask, jnp.exp(le - jnp.max(le, axis=-1, keepdims=True)), 0.0)
    pe = jnp.where(emask, ee / jnp.sum(ee, axis=-1, keepdims=True), -1.0)
    p1 = jnp.max(pe, axis=-1, keepdims=True)
    i1 = _first_lane_where(pe == p1, lane)
    pe2 = jnp.where(lane == i1, -1.0, pe)
    p2 = jnp.max(pe2, axis=-1, keepdims=True)
    i2 = _first_lane_where(pe2 == p2, lane)
    den = p1 + p2
    e1 = (i1 - MOE_GROUPS).astype(F32)
    e2 = (i2 - MOE_GROUPS).astype(F32)
    out = jnp.where(lane == 0, e1, jnp.where(lane == 1, e2,
          jnp.where(lane == 2, p_g * p1 / den, jnp.where(lane == 3, p_g * p2 / den, 0.0))))
    hit = (lane == i1 - MOE_GROUPS) | (lane == i2 - MOE_GROUPS)
    return out, hit


def _plan_kernel(n_tok, route_ref, pstart_ref, tok_ref, dst_ref, run_s, dvm_s, dsm_s, sem):
    i = pl.program_id(0)
    tm = route_ref.shape[0]
    cap = tok_ref.shape[0]
    unroll = 8

    @pl.when(i == 0)
    def _():
        run_s[...] = jnp.zeros(run_s.shape, F32)

        def clear(g, c):
            for j in range(unroll):
                tok_ref[g * unroll + j] = 0
                dst_ref[g * unroll + j] = 0
            return c
        lax.fori_loop(0, cap // unroll, clear, 0)

    route = route_ref[...]
    lane = lax.broadcasted_iota(I32, route.shape, 1).astype(F32)
    oh0 = jnp.where(lane == route[:, 0:1], 1.0, 0.0)
    oh1 = jnp.where(lane == route[:, 1:2], 1.0, 0.0)
    both = oh0 + oh1
    r_i = lax.broadcasted_iota(I32, (tm, tm), 0)
    c_i = lax.broadcasted_iota(I32, (tm, tm), 1)
    before = _dot(jnp.where(c_i < r_i, 1.0, 0.0).astype(BF16), both.astype(BF16))
    base = pstart_ref[...] + run_s[...] + before
    d0 = jnp.sum(oh0 * base, axis=-1, keepdims=True)
    d1 = jnp.sum(oh1 * base, axis=-1, keepdims=True)
    run_s[...] += jnp.sum(both, axis=0, keepdims=True)
    lane_i = lax.broadcasted_iota(I32, route.shape, 1)
    dcols = jnp.where(lane_i == 0, d0, jnp.where(lane_i == 1, d1, 0.0))
    dvm_s[...] = dcols.T[0:8, :].astype(I32)
    cp = pltpu.make_async_copy(dvm_s, dsm_s, sem)
    cp.start()
    cp.wait()

    def place(g, c):
        for j in range(unroll):
            r = g * unroll + j
            tok = i * tm + r
            s0 = dsm_s[0, r]
            s1 = dsm_s[1, r]
            tok_ref[s0] = tok
            dst_ref[s0] = tok
            tok_ref[s1] = tok
            dst_ref[s1] = n_tok + tok
        return c
    lax.fori_loop(0, tm // unroll, place, 0)


def _moe_plan(route, pad_start, cap, tm=1024):
    n = route.shape[0]
    tm = _tile(n, tm, LANES)
    smem_out = pl.BlockSpec(memory_space=pltpu.SMEM)
    return pl.pallas_call(
        functools.partial(_plan_kernel, n),
        out_shape=(jax.ShapeDtypeStruct((cap,), I32), jax.ShapeDtypeStruct((cap,), I32)),
        grid=(n // tm,),
        in_specs=[pl.BlockSpec((tm, LANES), lambda i: (i, 0)), pl.BlockSpec((1, LANES), lambda i: (0, 0))],
        out_specs=(smem_out, smem_out),
        scratch_shapes=[pltpu.VMEM((1, LANES), F32), pltpu.VMEM((8, tm), I32), pltpu.SMEM((8, tm), I32),
                        pltpu.SemaphoreType.DMA],
        compiler_params=_cparams("arbitrary"),
        name="moe_plan",
    )(route, pad_start)


def _expert_kernel(be_ref, nv_ref, tok_ref, dst_ref, nact_ref,
                   x_hbm, wg_ref, wu_ref, wd_ref, y_hbm,
                   xbuf, ybuf, wgb, wub, wdb, gsem, ssem):
    i = pl.program_id(0)
    nb = pl.num_programs(0)
    nact = nact_ref[0]
    blk = xbuf.shape[1]
    slot = i % 2
    group = 8

    def gather_copy(tok, sl, r, n=1):
        return pltpu.make_async_copy(x_hbm.at[pl.ds(tok, n), :], xbuf.at[sl, pl.ds(r, n), :], gsem.at[sl])

    def scatter_copy(dst, sl, r, n=1):
        return pltpu.make_async_copy(ybuf.at[sl, pl.ds(r, n), :], y_hbm.at[pl.ds(dst, n), :], ssem.at[sl])

    def for_rows(b, fn):
        n = nv_ref[b]
        full = n // group

        def grp(g, c):
            for j in range(group):
                fn(g * group + j)
            return c
        lax.fori_loop(0, full, grp, 0)

        def one(r, c):
            fn(r)
            return c
        lax.fori_loop(full * group, n, one, 0)

    def wait_rows(b, whole, row):
        @pl.when(nv_ref[b] == blk)
        def _():
            whole.wait()

        @pl.when(nv_ref[b] < blk)
        def _():
            for_rows(b, lambda r: row(r).wait())

    def start_gather(b, sl):
        for_rows(b, lambda r: gather_copy(tok_ref[b * blk + r], sl, r).start())

    def wait_gather(b, sl):
        wait_rows(b, gather_copy(0, sl, 0, blk), lambda r: gather_copy(0, sl, r))

    def start_scatter(b, sl):
        for_rows(b, lambda r: scatter_copy(dst_ref[b * blk + r], sl, r).start())

    def wait_scatter(b, sl):
        wait_rows(b, scatter_copy(0, sl, 0, blk), lambda r: scatter_copy(0, sl, r))

    @pl.when(i == 0)
    def _():
        xbuf[...] = jnp.zeros(xbuf.shape, xbuf.dtype)

        @pl.when(nact > 0)
        def _():
            start_gather(0, 0)

    @pl.when(i < nact)
    def _():
        wait_gather(i, slot)

        @pl.when(i + 1 < nact)
        def _():
            start_gather(i + 1, 1 - slot)

        prev = be_ref[jnp.maximum(i - 1, 0)]

        @pl.when((i == 0) | (be_ref[i] != prev))
        def _():
            wgb[...] = wg_ref[...].astype(BF16)
            wub[...] = wu_ref[...].astype(BF16)
            wdb[...] = wd_ref[...].astype(BF16)

        @pl.when(i >= 2)
        def _():
            wait_scatter(i - 2, slot)

        xb = xbuf[slot].astype(BF16)
        a = _dot(xb, wgb[...])
        u = _dot(xb, wub[...])
        h = (jax.nn.silu(a) * u).astype(BF16)
        ybuf[slot] = _dot(h, wdb[...])
        start_scatter(i, slot)

    @pl.when(i == nb - 1)
    def _():
        for back in (2, 1):
            b = nact - back

            @pl.when(b >= 0)
            def _():
                wait_scatter(b, b % 2)


def _moe_dispatch(route, counts_row, n_exp, blk):
    n = route.shape[0]
    n_asg = 2 * n
    experts = jnp.arange(n_exp, dtype=I32)
    counts = counts_row[0, :n_exp].astype(I32)
    padded = (counts + blk - 1) // blk * blk
    pad_end = jnp.cumsum(padded)
    pad_start = pad_end - padded
    n_blocks = (n_asg + n_exp * (blk - 1) + blk - 1) // blk
    block_start = jnp.arange(n_blocks, dtype=I32) * blk
    block_expert = jnp.minimum(jnp.sum(pad_end[None, :] <= block_start[:, None], axis=1, dtype=I32), n_exp - 1)
    mine = block_expert[:, None] == experts[None, :]
    pick = lambda v: jnp.sum(jnp.where(mine, v[None, :], 0), axis=1, dtype=I32)
    block_nvalid = jnp.clip(pick(counts) - (block_start - pick(pad_start)), 0, blk).astype(I32)
    n_active = (pad_end[-1] // blk).astype(I32).reshape(1)
    pstart_row = jnp.zeros((1, LANES), F32).at[0, :n_exp].set(pad_start.astype(F32))
    slot_tok, slot_dst = _moe_plan(route, pstart_row, n_blocks * blk)
    return block_expert, block_nvalid, slot_tok, slot_dst, n_active


def _moe_experts(x, route, counts_row, w_gate, w_up, w_down, layer):
    n, d = x.shape
    n_exp, ff = w_gate.shape[1], w_gate.shape[3]
    blk = MOE_BLOCK
    be, nv, slot_tok, slot_dst, nact = _moe_dispatch(route, counts_row, n_exp, blk)
    n_blocks = be.shape[0]
    grid_spec = pltpu.PrefetchScalarGridSpec(
        num_scalar_prefetch=5,
        grid=(n_blocks,),
        in_specs=[pl.BlockSpec(memory_space=pl.ANY),
                  pl.BlockSpec((None, None, d, ff), lambda i, be, *_: (layer, be[i], 0, 0)),
                  pl.BlockSpec((None, None, d, ff), lambda i, be, *_: (layer, be[i], 0, 0)),
                  pl.BlockSpec((None, None, ff, d), lambda i, be, *_: (layer, be[i], 0, 0))],
        out_specs=pl.BlockSpec(memory_space=pl.ANY),
        scratch_shapes=[pltpu.VMEM((2, blk, d), F32),
                        pltpu.VMEM((2, blk, d), F32),
                        pltpu.VMEM((d, ff), BF16), pltpu.VMEM((d, ff), BF16), pltpu.VMEM((ff, d), BF16),
                        pltpu.SemaphoreType.DMA((2,)), pltpu.SemaphoreType.DMA((2,))],
    )
    y = pl.pallas_call(
        _expert_kernel,
        out_shape=jax.ShapeDtypeStruct((2 * n, d), F32),
        grid_spec=grid_spec,
        compiler_params=_cparams("arbitrary"),
        name="moe_experts",
    )(be, nv, slot_tok, slot_dst, nact, x, w_gate, w_up, w_down)
    return y.reshape(2, n, d)


def _mixer_norm_moe(x, h, ln_g, ln_b, layer, alpha, rg_w, rg_b, re_w, re_b, w_gate, w_up, w_down):
    d = x.shape[1]
    n_r = MOE_GROUPS + MOE_GROUPS * MOE_EPG
    rw = jnp.zeros((d, LANES), F32).at[:, :MOE_GROUPS].set(rg_w[layer]).at[:, MOE_GROUPS:n_r].set(re_w[layer])
    rb = jnp.zeros((1, LANES), F32).at[0, :MOE_GROUPS].set(rg_b[layer]).at[0, MOE_GROUPS:n_r].set(re_b[layer])
    x1, route, counts_row = _ln_mixer(x, h, ln_g, ln_b, layer, 0, alpha, rw, rb)
    y2 = _moe_experts(x1, route, counts_row, w_gate, w_up, w_down, layer)
    return _ln_moe(x1, y2, route, ln_g, ln_b, layer, 1, alpha)


def _gdn_gate_kernel(nh, ab_ref, alog_ref, dtb_ref, gcb_ref, gct_ref):
    ab = ab_ref[...]
    tt = ab.shape[0]
    lane = lax.broadcasted_iota(I32, ab.shape, 1)
    pos = lax.broadcasted_iota(I32, ab.shape, 0) % GDN_CHUNK
    g = -jnp.exp(alog_ref[...]) * jax.nn.softplus(ab + dtb_ref[...])
    s = 1
    while s < GDN_CHUNK:
        g = g + jnp.where(pos >= s, pltpu.roll(g, s, 0), 0.0)
        s *= 2
    out = jnp.where(lane < nh, g, jnp.where(lane < 2 * nh, jax.nn.sigmoid(ab), 0.0))
    gcb_ref[...] = out
    for j in range(tt // LANES):
        gct_ref[j] = out[j * LANES:(j + 1) * LANES, :].T


def _gdn_gates(ab, alog_p, dtb_p, nh, tt=512):
    n = ab.shape[0]
    tt = _tile(n, tt, LANES)
    return pl.pallas_call(
        functools.partial(_gdn_gate_kernel, nh),
        out_shape=(jax.ShapeDtypeStruct((n, LANES), F32),
                   jax.ShapeDtypeStruct((n // LANES, LANES, LANES), F32)),
        grid=(n // tt,),
        in_specs=[pl.BlockSpec((tt, LANES), lambda i: (i, 0)),
                  pl.BlockSpec((1, LANES), lambda i: (0, 0)),
                  pl.BlockSpec((1, LANES), lambda i: (0, 0))],
        out_specs=(pl.BlockSpec((tt, LANES), lambda i: (i, 0)),
                   pl.BlockSpec((tt // LANES, LANES, LANES), lambda i: (i, 0, 0))),
        compiler_params=_cparams("parallel"),
        name="gdn_gates",
    )(ab, alog_p, dtb_p)


def _gdn_conv_kernel(hpb, ncb, x_ref, w_ref, o_ref, carry_ref):
    t = pl.program_id(1)
    j = pl.program_id(2)
    x = x_ref[...]
    tt, tc = x.shape

    @pl.when(t == 0)
    def _():
        carry_ref[j] = jnp.zeros((8, tc), F32)

    prev = jnp.tile(carry_ref[j], (tt // 8, 1))
    carry_ref[j] = x[tt - 8:tt, :]
    row = lax.broadcasted_iota(I32, x.shape, 0)
    acc = x * w_ref[GDN_CONV - 1:GDN_CONV, :]
    for s in range(1, GDN_CONV):
        xs = jnp.where(row < s, pltpu.roll(prev, s, 0), pltpu.roll(x, s, 0))
        acc = acc + xs * w_ref[GDN_CONV - 1 - s:GDN_CONV - s, :]
    y = jax.nn.silu(acc)
    kind = j // (ncb // 3)
    post = jnp.where(kind == 0, GDN_HEAD_DIM ** -0.5, 1.0)
    for hh in range(hpb):
        yh = y[:, hh * LANES:(hh + 1) * LANES]
        inv = lax.rsqrt(jnp.sum(yh * yh, axis=-1, keepdims=True) + RMS_EPS)
        o_ref[hh] = yh * (jnp.where(kind == 2, 1.0, inv) * post)


def _gdn_conv(proj, conv_w, layer, bsz, seq, nh, tt=256):
    hpb = _tile(nh, 8)
    tc = hpb * LANES
    ncb = 3 * nh // hpb
    hb = nh // hpb
    tt = _tile(seq, tt, 8)
    nt = seq // tt
    return pl.pallas_call(
        functools.partial(_gdn_conv_kernel, hpb, ncb),
        out_shape=jax.ShapeDtypeStruct((3, bsz, nh, seq, LANES), F32),
        grid=(bsz, nt, ncb),
        in_specs=[pl.BlockSpec((tt, tc), lambda b, t, j: (b * nt + t, j)),
                  pl.BlockSpec((None, GDN_CONV, tc), lambda b, t, j: (layer, 0, j))],
        out_specs=pl.BlockSpec((None, None, hpb, tt, LANES), lambda b, t, j: (j // hb, b, j % hb, t, 0)),
        scratch_shapes=[pltpu.VMEM((ncb, 8, tc), F32)],
        compiler_params=_cparams("arbitrary", "arbitrary", "arbitrary"),
        name="gdn_conv",
    )(proj, conv_w)


def _head_column(tile, lane, idx):
    return jnp.sum(jnp.where(lane == idx, tile, 0.0), axis=-1, keepdims=True)


def _gdn_local_kernel(nh, q_ref, k_ref, gcb_ref, gct_ref, l_ref, a_ref):
    h = pl.program_id(1)
    c = GDN_CHUNK
    lane = lax.broadcasted_iota(I32, (LANES, LANES), 1)
    row = lax.broadcasted_iota(I32, (LANES, LANES), 0)
    lower = ((row // c) == (lane // c)) & (lane <= row)
    for s in range(q_ref.shape[0] // LANES):
        rows = slice(s * LANES, (s + 1) * LANES)
        q = q_ref[rows, :]
        k = k_ref[rows, :]
        gcb = gcb_ref[rows, :]
        gcol = _head_column(gcb, lane, h)
        bcol = _head_column(gcb, lane, nh + h)
        grow = gct_ref[s, pl.ds(h, 1), :]
        decay = jnp.where(lower, jnp.exp(jnp.where(lower, gcol - grow, 0.0)), 0.0)
        kbf = k.astype(BF16)
        kk = _dot_nt((k * bcol).astype(BF16), kbf)
        qk = _dot_nt(q.astype(BF16), kbf)
        l_ref[s] = jnp.where(lane < row, kk * decay, 0.0)
        a_ref[s] = (qk * decay).astype(BF16)


def _gdn_local(qkv, gcb, gct, bsz, seq, nh, tt=2048):
    c = LANES
    tt = _tile(seq, tt, LANES)
    nt = seq // tt
    nc = seq // c
    qk_spec = lambda which: pl.BlockSpec((None, None, None, tt, LANES), lambda b, h, t: (which, b, h, t, 0))
    out_spec = pl.BlockSpec((None, None, tt // c, c, c), lambda b, h, t: (b, h, t, 0, 0))
    return pl.pallas_call(
        functools.partial(_gdn_local_kernel, nh),
        out_shape=(jax.ShapeDtypeStruct((bsz, nh, nc, c, c), F32),
                   jax.ShapeDtypeStruct((bsz, nh, nc, c, c), BF16)),
        grid=(bsz, nh, nt),
        in_specs=[qk_spec(0), qk_spec(1),
                  pl.BlockSpec((tt, LANES), lambda b, h, t: (b * nt + t, 0)),
                  pl.BlockSpec((tt // LANES, LANES, LANES), lambda b, h, t: (b * nt + t, 0, 0))],
        out_specs=(out_spec, out_spec),
        compiler_params=_cparams("parallel", "parallel", "parallel"),
        name="gdn_local",
    )(qkv, qkv, gcb, gct)


def _gdn_solve_kernel(l_ref, t_ref, lt_s, tt_s):
    c = GDN_CHUNK
    pr = 2 * c
    npb = l_ref.shape[0] // pr
    for i in range(c):
        both = l_ref[pl.ds(i, npb, stride=pr), :] + l_ref[pl.ds(c + i, npb, stride=pr), :]
        bt = both.T
        lt_s[0, i] = bt[:c, :]
        lt_s[1, i] = bt[c:, :]

    halves = (0, 1)
    for g in halves:
        tt_s[g, 0] = jnp.where(lax.broadcasted_iota(I32, (c, npb), 0) == 0, 1.0, 0.0)
    for ib in range(c // 8):
        ext = 8 * (ib + 1)
        sub = lax.broadcasted_iota(I32, (ext, npb), 0)

        def row_body(ii, carry, ib=ib, ext=ext, sub=sub):
            i = ib * 8 + ii
            accs = [jnp.where(sub == i, 1.0, 0.0) for _ in halves]
            for mb in range(ib):
                em = 8 * (mb + 1)
                for g in halves:
                    t = sum(lt_s[g, i, m:m + 1, :] * tt_s[g, m, 0:em, :] for m in range(8 * mb, em))
                    accs[g] = jnp.concatenate([accs[g][:em] - t, accs[g][em:]], axis=0)

            def m_body(mm, a):
                m = ib * 8 + mm
                return tuple(a[g] - lt_s[g, i, pl.ds(m, 1), :] * tt_s[g, m, 0:ext, :] for g in halves)
            accs = lax.fori_loop(0, ii, m_body, tuple(accs))
            for g in halves:
                tt_s[g, i, 0:ext, :] = accs[g]
                if ext < c:
                    tt_s[g, i, ext:c, :] = jnp.zeros((c - ext, npb), F32)
            return carry

        lax.fori_loop(1 if ib == 0 else 0, 8, row_body, 0)

    lane = lax.broadcasted_iota(I32, (npb, pr), 1)
    for i in range(c):
        rows = jnp.concatenate([tt_s[0, i], tt_s[1, i]], axis=0).T
        t_ref[pl.ds(i, npb, stride=pr), :] = jnp.where(lane < c, rows, 0.0)
        t_ref[pl.ds(c + i, npb, stride=pr), :] = jnp.where(lane >= c, rows, 0.0)


def _gdn_solve(lpair, pairs_per_step=128):
    c = GDN_CHUNK
    pr = 2 * c
    n_pairs = lpair.shape[0] // pr
    npb = _tile(n_pairs, pairs_per_step)
    spec = pl.BlockSpec((npb * pr, pr), lambda i: (i, 0))
    return pl.pallas_call(
        _gdn_solve_kernel,
        out_shape=jax.ShapeDtypeStruct(lpair.shape, F32),
        grid=(n_pairs // npb,),
        in_specs=[spec],
        out_specs=spec,
        scratch_shapes=[pltpu.VMEM((2, c, c, npb), F32), pltpu.VMEM((2, c, c, npb), F32)],
        compiler_params=_cparams("parallel"),
        name="gdn_solve",
    )(lpair)


def _gdn_recur_kernel(nh, q_ref, k_ref, v_ref, z_ref, gcb_ref, tm_ref, at_ref, ng_ref, o_ref, st_ref):
    hblk = pl.program_id(1)
    c = GDN_CHUNK
    pr = 2 * c
    hb, tt, dv = v_ref.shape
    lane = lax.broadcasted_iota(I32, (c, LANES), 1)

    @pl.when(pl.program_id(2) == 0)
    def _():
        st_ref[...] = jnp.zeros(st_ref.shape, F32)

    def body(pi, carry):
        for half in range(2):
            rows = pl.ds(pl.multiple_of(pi * pr, pr) + half * c, c)
            blk = slice(half * c, (half + 1) * c)
            gcb = gcb_ref[rows, :]
            for j in range(hb):
                h = hblk * hb + j
                q = q_ref[j, rows, :]
                k = k_ref[j, rows, :]
                v = v_ref[j, rows, :]
                gcol = _head_column(gcb, lane, h)
                bcol = _head_column(gcb, lane, nh + h)
                glast = gcol[c - 1:c, :]
                eg = jnp.exp(gcol)
                kb = k * bcol
                sol = _dot3(tm_ref[j, pi, blk, blk], jnp.concatenate([v * bcol, kb * eg], axis=-1))
                u = sol[:, :dv]
                w = sol[:, dv:]
                state = st_ref[j]
                sb = state.astype(BF16)
                v_new = u - _dot(w.astype(BF16), sb)
                vnb = v_new.astype(BF16)
                o = _dot((q * eg).astype(BF16), sb) + _dot(at_ref[j, pi, blk, blk], vnb)
                kd = k * jnp.exp(glast - gcol)
                st_ref[j] = state * jnp.exp(glast) + _dot_tn(kd.astype(BF16), vnb)
                o = o * lax.rsqrt(jnp.mean(o * o, axis=-1, keepdims=True) + RMS_EPS) * ng_ref[...]
                o = o * jax.nn.silu(z_ref[rows, j * dv:(j + 1) * dv])
                o_ref[rows, j * dv:(j + 1) * dv] = o.astype(o_ref.dtype)
        return carry

    lax.fori_loop(0, tt // pr, body, 0)


def _gdn_recur(qkv, proj, gcb, tm, at, norm_g, layer, bsz, seq, nh, hb=16, tt=256):
    c = 2 * GDN_CHUNK
    hb = _tile(nh, hb)
    tt = _tile(seq, tt, c)
    nt = seq // tt
    head = lambda which: pl.BlockSpec((None, None, hb, tt, LANES), lambda b, h, t: (which, b, h, t, 0))
    mat = pl.BlockSpec((None, hb, tt // c, c, c), lambda b, h, t: (b, h, t, 0, 0))
    return pl.pallas_call(
        functools.partial(_gdn_recur_kernel, nh),
        out_shape=jax.ShapeDtypeStruct((bsz * seq, nh * LANES), BF16),
        grid=(bsz, nh // hb, nt),
        in_specs=[head(0), head(1), head(2),
                  pl.BlockSpec((tt, hb * LANES), lambda b, h, t: (b * nt + t, 3 * nh // hb + h)),
                  pl.BlockSpec((tt, LANES), lambda b, h, t: (b * nt + t, 0)),
                  mat, mat,
                  pl.BlockSpec((None, 1, LANES), lambda b, h, t: (layer, 0, 0))],
        out_specs=pl.BlockSpec((tt, hb * LANES), lambda b, h, t: (b * nt + t, h)),
        scratch_shapes=[pltpu.VMEM((hb, GDN_HEAD_DIM, LANES), F32)],
        compiler_params=_cparams("parallel", "parallel", "arbitrary"),
        name="gdn_recur",
    )(qkv, qkv, qkv, proj, gcb, tm, at, norm_g.reshape(-1, 1, LANES))


def _pad_lanes(v):
    return jnp.zeros((1, LANES), F32).at[0, :v.shape[0]].set(v)


def _gdn_mixer(x_bf, bsz, seq, w_in, conv_w, a_log, dt_bias, norm_g, w_out, layer):
    n, d = x_bf.shape
    nh = d // GDN_HEAD_DIM
    c = GDN_CHUNK
    proj = _matmul(x_bf, w_in, layer, 4 * d, tn=1024, name="gdn_in")
    w_ab = jnp.zeros((1, d, LANES), F32).at[0, :, :2 * nh].set(w_in[layer, :, 4 * d:])
    ab = _matmul(x_bf, w_ab, 0, LANES, name="gdn_in_ab")
    gcb, gct = _gdn_gates(ab, _pad_lanes(a_log[layer]), _pad_lanes(dt_bias[layer]), nh)
    qkv = _gdn_conv(proj, conv_w, layer, bsz, seq, nh)
    lpair, at = _gdn_local(qkv, gcb, gct, bsz, seq, nh)
    tm = _gdn_solve(lpair.reshape(-1, 2 * c)).reshape(lpair.shape)
    og = _gdn_recur(qkv, proj, gcb, tm, at, norm_g, layer, bsz, seq, nh)
    return _matmul(og, w_out, layer, d, name="gdn_out")


def _s5_matrices(b_re, b_im, c_re, c_im, a_re, a_im, log_dt):
    lc = SSM_CHUNK
    g, p, nch = b_re.shape
    dt = jnp.exp(log_dt)[:, None]
    mag = jnp.exp(a_re * dt)
    ang = a_im * dt
    lb_re = mag * jnp.cos(ang)
    lb_im = mag * jnp.sin(ang)
    den = jnp.square(a_re) + jnp.square(a_im)
    f_re = ((lb_re - 1.0) * a_re + lb_im * a_im) / den
    f_im = (lb_im * a_re - (lb_re - 1.0) * a_im) / den
    bb_re = f_re[..., None] * b_re - f_im[..., None] * b_im
    bb_im = f_re[..., None] * b_im + f_im[..., None] * b_re
    dd = jnp.arange(lc + 1, dtype=F32)[:, None, None]
    pw_mag = jnp.exp(dd * (a_re * dt))
    pw_re = pw_mag * jnp.cos(dd * ang)
    pw_im = pw_mag * jnp.sin(dd * ang)
    gpb = LANES // nch
    nb = g // gpb
    wide = gpb * p
    rows = lambda a: a.reshape(nb, LANES, p)
    lanes = lambda a: jnp.transpose(a.reshape(lc + 1, nb, wide), (1, 0, 2))
    bt_re = rows(jnp.transpose(bb_re, (0, 2, 1)))
    bt_im = rows(jnp.transpose(bb_im, (0, 2, 1)))
    pw_re, pw_im = lanes(pw_re), lanes(pw_im)
    width = lc * LANES
    per_block = lambda a: pl.BlockSpec((None,) + a.shape[1:], lambda i: (i, 0, 0))
    ins = (rows(c_re), rows(c_im), bt_re, bt_im, pw_re, pw_im)
    out = jax.ShapeDtypeStruct((nb, width, width), BF16)
    assert 2 * wide == width
    mcat, qcat, pcat = pl.pallas_call(
        functools.partial(_s5_operator_kernel, lc),
        out_shape=(out, out, out),
        grid=(nb,),
        in_specs=[per_block(a) for a in ins],
        out_specs=(per_block(out), per_block(out), per_block(out)),
        compiler_params=_cparams("parallel"),
        name="s5_operators",
    )(*ins)
    a_pow = jnp.stack([pw_re[:, lc], pw_im[:, lc]], axis=1)
    return mcat, qcat, pcat, a_pow


def _dot3_nt(a, b):
    ah, al = _split_bf16(a)
    bh, bl = _split_bf16(b)
    return _dot_nt(ah, bh) + (_dot_nt(ah, bl) + _dot_nt(al, bh))


def _s5_operator_kernel(lc, c_re_ref, c_im_ref, b_re_ref, b_im_ref, pw_re_ref, pw_im_ref, m_ref, q_ref, p_ref):
    p = c_re_ref.shape[1]
    wide = pw_re_ref.shape[1]
    row = lax.broadcasted_iota(I32, (LANES, wide), 0)
    lane = lax.broadcasted_iota(I32, (LANES, wide), 1)
    own = (row // SSM_GROUP) == (lane // p)

    def embed(ref):
        return jnp.where(own, jnp.concatenate([ref[...]] * (wide // p), axis=1), 0.0)

    c_re, c_im, b_re, b_im = embed(c_re_ref), embed(c_im_ref), embed(b_re_ref), embed(b_im_ref)
    taps = []
    for d in range(lc + 1):
        pr = pw_re_ref[d:d + 1, :]
        pi = pw_im_ref[d:d + 1, :]
        cp_re = c_re * pr - c_im * pi
        cp_im = c_re * pi + c_im * pr
        if d < lc:
            taps.append(_dot3_nt(b_re, cp_re) - _dot3_nt(b_im, cp_im))
            s = lc - 1 - d
            q_ref[s * LANES:(s + 1) * LANES, :wide] = (b_re * pr - b_im * pi).astype(BF16)
            q_ref[s * LANES:(s + 1) * LANES, wide:] = (b_re * pi + b_im * pr).astype(BF16)
        if d > 0:
            t = d - 1
            p_ref[:wide, t * LANES:(t + 1) * LANES] = cp_re.T.astype(BF16)
            p_ref[wide:, t * LANES:(t + 1) * LANES] = (-cp_im).T.astype(BF16)
    zero = jnp.zeros((LANES, LANES), BF16)
    for s in range(lc):
        for t in range(lc):
            m_ref[s * LANES:(s + 1) * LANES, t * LANES:(t + 1) * LANES] = (
                taps[t - s].astype(BF16) if t >= s else zero)


def _s5_scan_kernel(lc, nck, u_ref, m_ref, q_ref, p_ref, a_ref, d_ref, v_ref, x_s, h_s):
    r = u_ref.shape[0] // lc
    bl = r // nck
    nhalf = x_s.shape[0] // 2
    step = lambda s: pl.ds(s, r, stride=lc)
    ucat = jnp.concatenate([u_ref[step(s), :] for s in range(lc)], axis=-1).astype(BF16)
    x = _dot(ucat, q_ref[...])
    for j in range(2 * nhalf):
        x_s[j] = x[:, j * LANES:(j + 1) * LANES]
    ar = [a_ref[0:1, j * LANES:(j + 1) * LANES] for j in range(nhalf)]
    ai = [a_ref[1:2, j * LANES:(j + 1) * LANES] for j in range(nhalf)]

    def body(ci, carry):
        rows = pl.ds(ci, bl, stride=nck)
        out = []
        for j in range(nhalf):
            sr, si = carry[j], carry[nhalf + j]
            h_s[j, rows, :] = sr
            h_s[nhalf + j, rows, :] = si
            out.append((ar[j] * sr - ai[j] * si + x_s[j, rows, :],
                        ar[j] * si + ai[j] * sr + x_s[nhalf + j, rows, :]))
        return tuple(o[0] for o in out) + tuple(o[1] for o in out)

    lax.fori_loop(0, nck, body, tuple(jnp.zeros((bl, LANES), F32) for _ in range(2 * nhalf)))
    h = jnp.concatenate([h_s[j] for j in range(2 * nhalf)], axis=-1).astype(BF16)
    y = _dot(ucat, m_ref[...]) + _dot(h, p_ref[...])
    for s in range(lc):
        v_ref[step(s), :] = jax.nn.gelu(y[:, s * LANES:(s + 1) * LANES] + d_ref[...] * u_ref[step(s), :])


def _s5_scan(u, mats, d_skip, layer, bsz, seq):
    n, w = u.shape
    lc = SSM_CHUNK
    nck = seq // lc
    mcat, qcat, pcat, a_pow = mats
    nb = mcat.shape[0]
    bl = _tile(bsz, 4)
    rows = bl * seq
    r = rows // lc
    per_block = lambda a: pl.BlockSpec((None,) + a.shape[1:], lambda i, j: (i,) + (0,) * (a.ndim - 1))
    io = pl.BlockSpec((rows, LANES), lambda i, j: (j, i))
    return pl.pallas_call(
        functools.partial(_s5_scan_kernel, lc, nck),
        out_shape=jax.ShapeDtypeStruct((n, w), F32),
        grid=(nb, bsz // bl),
        in_specs=[io, per_block(mcat), per_block(qcat), per_block(pcat), per_block(a_pow),
                  pl.BlockSpec((None, 1, LANES), lambda i, j: (layer, 0, i))],
        out_specs=io,
        scratch_shapes=[pltpu.VMEM((qcat.shape[2] // LANES, r, LANES), F32)] * 2,
        compiler_params=_cparams("parallel", "parallel"),
        name="s5_scan",
    )(u, mcat, qcat, pcat, a_pow, d_skip.reshape(-1, 1, w))


def _s5_glu_kernel(vrow_ref, v_ref, w_ref, b_ref, o_ref, wbf_ref):
    @pl.when(pl.program_id(1) == 0)
    def _():
        wbf_ref[...] = w_ref[...].astype(BF16)

    z = _dot(vrow_ref[...].astype(BF16), wbf_ref[...]) + b_ref[...]
    o_ref[...] = (v_ref[...] * jax.nn.sigmoid(z)).astype(o_ref.dtype)


def _s5_glu(v, w_glu, b_glu, layer, tm=1024, tn=512):
    n, w = v.shape
    tm = _tile(n, tm)
    tn = _tile(w, tn, LANES)
    return pl.pallas_call(
        _s5_glu_kernel,
        out_shape=jax.ShapeDtypeStruct((n, w), BF16),
        grid=(w // tn, n // tm),
        in_specs=[pl.BlockSpec((tm, w), lambda j, i: (i, 0)),
                  pl.BlockSpec((tm, tn), lambda j, i: (i, j)),
                  pl.BlockSpec((None, w, tn), lambda j, i: (layer, 0, j)),
                  pl.BlockSpec((None, 1, tn), lambda j, i: (layer, 0, j))],
        out_specs=pl.BlockSpec((tm, tn), lambda j, i: (i, j)),
        scratch_shapes=[pltpu.VMEM((w, tn), BF16)],
        compiler_params=_cparams("arbitrary", "arbitrary"),
        name="s5_glu",
    )(v, v, w_glu, b_glu.reshape(-1, 1, w))


def _s5_mixer(x_bf, bsz, seq, w_in, b_re, b_im, c_re, c_im, a_re, a_im, log_dt, d_skip,
              w_glu, b_glu, w_out, layer):
    n, d = x_bf.shape
    w = w_in.shape[2]
    u = _matmul(x_bf, w_in, layer, w, name="s5_in")
    mats = _s5_matrices(b_re[layer], b_im[layer], c_re[layer], c_im[layer], a_re[layer], a_im[layer], log_dt[layer])
    v = _s5_scan(u, mats, d_skip, layer, bsz, seq)
    hg = _s5_glu(v, w_glu, b_glu, layer)
    return _matmul(hg, w_out, layer, d, name="s5_out")


INT_MIN = -2 ** 31
SUM_ROWS = 16


def _fold_rows(x, op):
    return functools.reduce(op, [x[r:r + 8, :] for r in range(0, x.shape[0], 8)])


def _dsa_kernel(nsel, q_ref, qi_ref, wi_ref, k_ref, v_ref, ki_ref, o_ref,
                kbf, vext_t, kibf, qis, qs, w_t, slope_s, keys, selm, s_s, acc_s):
    qb = pl.program_id(1)
    blk = Q_BLOCK
    hd = ATT_HEAD_DIM
    nah = q_ref.shape[1] // hd
    nkb = qb + 1

    @pl.when(qb == 0)
    def _():
        kbf[...] = k_ref[...].astype(BF16)
        kibf[...] = ki_ref[...].astype(BF16)
        for j in range(vext_t.shape[0]):
            vext_t[j, :hd, :] = v_ref[j * blk:(j + 1) * blk, :].T.astype(BF16)
            vext_t[j, hd:, :] = jnp.ones((vext_t.shape[1] - hd, blk), BF16)
        for h in range(nah):
            slope_s[:, h * blk:(h + 1) * blk] = jnp.full((8, blk), 2.0 ** (-8.0 * (h + 1) / nah), F32)

    qi = qi_ref[...]
    for h in range(IDX_HEADS):
        qis[h * blk:(h + 1) * blk, :] = qi[:, h * IDX_DIM:(h + 1) * IDX_DIM].astype(BF16)
    q = q_ref[...]
    for h in range(nah):
        qs[h * blk:(h + 1) * blk, :] = q[:, h * hd:(h + 1) * hd].astype(BF16)
    w_t[...] = (wi_ref[...] * (IDX_HEADS ** -0.5 * IDX_DIM ** -0.5)).T

    row = lax.broadcasted_iota(I32, (blk, blk), 0)
    col = lax.broadcasted_iota(I32, (blk, blk), 1)
    tpos = qb * blk + col

    def score_body(kb, carry):
        off = pl.multiple_of(kb * blk, blk)
        dots = _dot_nt(kibf[pl.ds(off, blk), :], qis[...])
        sc = jnp.zeros((blk, blk), F32)
        for h in range(IDX_HEADS):
            sc = sc + jnp.maximum(dots[:, h * blk:(h + 1) * blk], 0.0) * w_t[h:h + 1, :]
        sc = jnp.where(sc == 0.0, 0.0, sc)
        bits = pltpu.bitcast(sc, I32)
        key = jnp.where(bits < 0, bits ^ 0x7FFFFFFF, bits)
        keys[kb] = jnp.where(off + row <= tpos, key, INT_MIN)
        return carry

    lax.fori_loop(0, nkb, score_body, 0)

    def count_ge(cands):
        def body(kb, accs):
            key = keys[kb]
            return tuple(acc + jnp.where(key >= cand, 1.0, 0.0) for acc, cand in zip(accs, cands))
        accs = lax.fori_loop(0, nkb, body, tuple(jnp.zeros((blk, blk), F32) for _ in cands))
        return [jnp.sum(_fold_rows(acc, jnp.add), axis=0, keepdims=True) for acc in accs]

    def bit_body(t, th):
        cand = th + jnp.left_shift(jnp.int32(1), 31 - t)
        return jnp.where(count_ge([cand])[0] >= nsel, cand, th)

    th = lax.fori_loop(0, 32, bit_body, jnp.full((1, blk), INT_MIN, I32))
    n_gt, n_ge = count_ge([th + 1, th])
    overflow = jnp.where((n_ge > nsel) & (th > INT_MIN), 1.0, 0.0)
    has_ties = jnp.max(overflow) > 0.0

    @pl.when(jnp.logical_not(has_ties))
    def _():
        floor = jnp.maximum(th, INT_MIN + 1)

        def body(kb, carry):
            selm[kb] = jnp.where(keys[kb] >= floor, 1.0, 0.0)
            return carry
        lax.fori_loop(0, nkb, body, 0)

    @pl.when(has_ties)
    def _():
        need = nsel - n_gt
        tri = jnp.where(col <= row, 1.0, 0.0).astype(BF16)

        def body(kb, run):
            key = keys[kb]
            eq = jnp.where(key == th, 1.0, 0.0)
            rank = run + _dot(tri, eq.astype(BF16))
            take = (key > th) | ((key == th) & (rank <= need))
            selm[kb] = jnp.where(take & (key > INT_MIN), 1.0, 0.0)
            return run + jnp.sum(eq, axis=0, keepdims=True)
        lax.fori_loop(0, nkb, body, jnp.zeros((1, blk), F32))

    kcol = lax.broadcasted_iota(I32, (blk, 1), 0)

    def max_body(kb, m8):
        off = pl.multiple_of(kb * blk, blk)
        kpos = (off + kcol).astype(F32)
        keep = jnp.tile(selm[kb], (1, nah)) > 0.5
        s = _dot_nt(kbf[pl.ds(off, blk), :], qs[...]) * (hd ** -0.5) + kpos * slope_s[0:1, :]
        s = jnp.where(keep, s, NEG_INF)
        s_s[kb] = s
        return jnp.maximum(m8, _fold_rows(s, jnp.maximum))

    m8 = lax.fori_loop(0, nkb, max_body, jnp.full((8, nah * blk), NEG_INF, F32))
    m = jnp.max(m8, axis=0, keepdims=True)
    acc_s[...] = jnp.zeros(acc_s.shape, F32)

    def acc_body(kb, carry):
        p = jnp.exp(s_s[kb] - m).astype(BF16)
        acc_s[...] += _dot(vext_t[kb], p)
        return carry

    lax.fori_loop(0, nkb, acc_body, 0)
    out_t = acc_s[:hd, :] / acc_s[hd:hd + 1, :]
    for h in range(nah):
        o_ref[:, h * hd:(h + 1) * hd] = out_t[:, h * blk:(h + 1) * blk].T.astype(o_ref.dtype)


def _dsa_attention(proj, bsz, seq, d):
    blk = Q_BLOCK
    nq = seq // blk
    nah = d // ATT_HEAD_DIM
    iw = IDX_HEADS * IDX_DIM
    assert iw % d == 0
    kcol = (iw + d) // LANES
    nsel = float(min(TOPK_MAX, seq // 4))
    rowblk = lambda width, cidx: pl.BlockSpec((blk, width), lambda b, i: (b * nq + i, cidx))
    seqblk = lambda cidx: pl.BlockSpec((seq, LANES), lambda b, i: (b, cidx))
    return pl.pallas_call(
        functools.partial(_dsa_kernel, nsel),
        out_shape=jax.ShapeDtypeStruct((bsz * seq, d), BF16),
        grid=(bsz, nq),
        in_specs=[rowblk(d, iw // d), rowblk(iw, 0), rowblk(LANES, kcol + 3),
                  seqblk(kcol), seqblk(kcol + 1), seqblk(kcol + 2)],
        out_specs=pl.BlockSpec((blk, d), lambda b, i: (b * nq + i, 0)),
        scratch_shapes=[pltpu.VMEM((seq, ATT_HEAD_DIM), BF16),
                        pltpu.VMEM((nq, ATT_HEAD_DIM + SUM_ROWS, blk), BF16),
                        pltpu.VMEM((seq, IDX_DIM), BF16),
                        pltpu.VMEM((IDX_HEADS * blk, IDX_DIM), BF16),
                        pltpu.VMEM((nah * blk, ATT_HEAD_DIM), BF16),
                        pltpu.VMEM((LANES, blk), F32),
                        pltpu.VMEM((8, nah * blk), F32),
                        pltpu.VMEM((nq, blk, blk), I32),
                        pltpu.VMEM((nq, blk, blk), F32),
                        pltpu.VMEM((nq, blk, nah * blk), F32),
                        pltpu.VMEM((ATT_HEAD_DIM + SUM_ROWS, nah * blk), F32)],
        compiler_params=_cparams("parallel", "arbitrary"),
        name="dsa_attention",
    )(proj, proj, proj, proj, proj, proj)


def _dsa_mixer(x_bf, bsz, seq, w_in, w_out, layer):
    n, d = x_bf.shape
    hd = ATT_HEAD_DIM
    iw = IDX_HEADS * IDX_DIM
    w = w_in[layer]
    o_k, o_v, o_qi, o_ki, o_wi = d, d + hd, d + 2 * hd, d + 2 * hd + iw, d + 2 * hd + iw + IDX_DIM
    cols = [w[:, o_qi:o_ki], w[:, :o_k], w[:, o_k:o_v], w[:, o_v:o_qi], w[:, o_ki:o_wi], w[:, o_wi:]]
    width = sum(c.shape[1] for c in cols)
    pad = -(-width // (4 * LANES)) * (4 * LANES) - width
    w_re = jnp.concatenate(cols + [jnp.zeros((d, pad), F32)], axis=1)[None]
    proj = _matmul(x_bf, w_re, 0, width + pad, name="dsa_in")
    o = _dsa_attention(proj, bsz, seq, d)
    return _matmul(o, w_out, layer, d, name="dsa_out")


def kernel(x, ln_g, ln_b, moe_rg_w, moe_rg_b, moe_re_w, moe_re_b, moe_w_gate, moe_w_up, moe_w_down,
           gdn_w_in, gdn_conv_w, gdn_a_log, gdn_dt_bias, gdn_norm_g, gdn_w_out,
           ssm_w_in, ssm_b_re, ssm_b_im, ssm_c_re, ssm_c_im, ssm_a_re, ssm_a_im, ssm_log_dt,
           ssm_d, ssm_w_glu, ssm_b_glu, ssm_w_out, dsa_w_in, dsa_w_out):
    bsz, seq, d = x.shape
    depth = ln_g.shape[0]
    alpha = (2.0 * depth) ** 0.25
    xf = x.reshape(bsz * seq, d)
    xb = xf.astype(BF16)
    counts = [0, 0, 0]
    for layer in range(depth):
        kind = layer % 3
        i = counts[kind]
        counts[kind] += 1
        if kind == 0:
            h = _gdn_mixer(xb, bsz, seq, gdn_w_in, gdn_conv_w, gdn_a_log, gdn_dt_bias, gdn_norm_g, gdn_w_out, i)
        elif kind == 1:
            h = _s5_mixer(xb, bsz, seq, ssm_w_in, ssm_b_re, ssm_b_im, ssm_c_re, ssm_c_im, ssm_a_re, ssm_a_im,
                          ssm_log_dt, ssm_d, ssm_w_glu, ssm_b_glu, ssm_w_out, i)
        else:
            h = _dsa_mixer(xb, bsz, seq, dsa_w_in, dsa_w_out, i)
        xf, xb = _mixer_norm_moe(xf, h, ln_g, ln_b, layer, alpha, moe_rg_w, moe_rg_b, moe_re_w, moe_re_b,
                                 moe_w_gate, moe_w_up, moe_w_down)
    return xf.reshape(bsz, seq, d)
```
